```python
import jax, jax.numpy as jnp
from jax import lax
import numpy as np

D_MODEL = 1024
BATCH = 2
SEQ = 16384
DEPTH = 1

CHUNK = 64
N_META = 16
META_PAD = CHUNK - N_META
Q_BLOCK = 128
FOX_HEADS = 8
FOX_HEAD_DIM = 64
FOX_WIDTH = FOX_HEADS * FOX_HEAD_DIM
GDN_HEADS = 8
GDN_HEAD_DIM = 64
GDN_WIDTH = GDN_HEADS * GDN_HEAD_DIM
GDN_CONV = 4
D_FF = 2816
FFN_CONV = 3
IN_SPLITS = (3 * FOX_WIDTH, FOX_HEADS, 3 * GDN_WIDTH, GDN_WIDTH, GDN_HEADS, GDN_HEADS, 2 * D_MODEL)
IN_WIDTH = 3 * FOX_WIDTH + FOX_HEADS + 4 * GDN_WIDTH + 2 * GDN_HEADS + 2 * D_MODEL
RMS_EPS = 1e-6
NEG_INF = -1e30

kernel_name = "hybrid_fox_gdn_convffn_meta"


def rms_norm(x, gain):
    xf = x.astype(jnp.float32)
    y = xf * lax.rsqrt(jnp.mean(xf * xf, axis=-1, keepdims=True) + RMS_EPS)
    return (y * gain.astype(jnp.float32)).astype(x.dtype)


def l2_norm(x):
    return x * lax.rsqrt(jnp.sum(x * x, axis=-1, keepdims=True) + RMS_EPS)


def causal_dwconv(x, w, b=None):
    k_width, ch = w.shape
    y = lax.conv_general_dilated(
        x, w[:, None, :].astype(x.dtype), window_strides=(1,), padding=[(k_width - 1, 0)],
        dimension_numbers=('NWC', 'WIO', 'NWC'), feature_group_count=ch)
    if b is not None:
        y = y + b.astype(x.dtype)
    return y


def split_cols(a, sizes):
    idx = []
    acc = 0
    for s in sizes[:-1]:
        acc += s
        idx.append(acc)
    return jnp.split(a, idx, axis=-1)


def forgetting_attention(q, k, v, log_f):
    bn, seq_len, heads, dh = q.shape
    lp = -(-seq_len // Q_BLOCK) * Q_BLOCK
    pad = lp - seq_len
    q, k, v = [jnp.pad(a, ((0, 0), (0, pad), (0, 0), (0, 0))) for a in (q, k, v)]
    c = jnp.cumsum(jnp.pad(log_f, ((0, 0), (0, pad), (0, 0))), axis=1).transpose(0, 2, 1)
    nblk = lp // Q_BLOCK
    qb = q.reshape(bn, nblk, Q_BLOCK, heads, dh).transpose(1, 0, 3, 2, 4)
    cb = c.reshape(bn, heads, nblk, Q_BLOCK).transpose(2, 0, 1, 3)
    kpos = jnp.arange(lp)
    scale = dh ** -0.5

    def block(args):
        q_i, c_i, i = args
        s = jnp.einsum('bhqd,bkhd->bhqk', q_i, k, preferred_element_type=jnp.float32) * scale
        s = s + c_i[..., None] - c[:, :, None, :]
        qpos = i * Q_BLOCK + jnp.arange(Q_BLOCK)
        s = jnp.where(kpos[None, :] <= qpos[:, None], s, NEG_INF)
        p = jax.nn.softmax(s, axis=-1).astype(v.dtype)
        return jnp.einsum('bhqk,bkhd->bqhd', p, v)

    o = lax.map(block, (qb, cb, jnp.arange(nblk)))
    return o.transpose(1, 0, 2, 3, 4).reshape(bn, lp, heads * dh)[:, :seq_len]


def gated_delta_rule(q, k, v, beta, g):
    bn, seq_len, heads, dk = q.shape
    dv = v.shape[-1]
    q = l2_norm(q.astype(jnp.float32)) * (dk ** -0.5)
    k = l2_norm(k.astype(jnp.float32))
    v = v.astype(jnp.float32)
    total = META_PAD + seq_len
    back = (-total) % CHUNK
    total = total + back
    n_chunks = total // CHUNK
    pad4 = ((0, 0), (META_PAD, back), (0, 0), (0, 0))
    pad3 = ((0, 0), (META_PAD, back), (0, 0))
    q, k, v = [jnp.pad(a, pad4).reshape(bn, n_chunks, CHUNK, heads, -1).transpose(0, 3, 1, 2, 4)
               for a in (q, k, v)]
    beta, g = [jnp.pad(a, pad3).reshape(bn, n_chunks, CHUNK, heads).transpose(0, 3, 1, 2)
               for a in (beta, g)]
    gc = jnp.cumsum(g, axis=-1)
    tril = jnp.tril(jnp.ones((CHUNK, CHUNK), dtype=bool))
    strict = jnp.tril(jnp.ones((CHUNK, CHUNK), dtype=bool), -1)
    decay = jnp.exp(jnp.where(tril, gc[..., :, None] - gc[..., None, :], NEG_INF))
    kb = k * beta[..., None]
    vb = v * beta[..., None]
    lmat = jnp.where(strict, jnp.einsum('bhncd,bhnsd->bhncs', kb, k) * decay, 0.0)
    eye = jnp.eye(CHUNK, dtype=jnp.float32)
    rhs = jnp.concatenate([vb, kb * jnp.exp(gc)[..., None]], axis=-1)
    sol = lax.linalg.triangular_solve(lmat + eye, rhs, left_side=True, lower=True, unit_diagonal=True)
    value, k_cumdecay = sol[..., :dv], sol[..., dv:]
    attn_intra = jnp.einsum('bhncd,bhnsd->bhncs', q, k) * decay
    q_dec = q * jnp.exp(gc)[..., None]
    k_tail = k * jnp.exp(gc[..., -1:] - gc)[..., None]
    chunk_decay = jnp.exp(gc[..., -1])

    def step(state, xs):
        value_n, kcd_n, attn_n, qd_n, kt_n, cd_n = xs
        v_new = value_n - jnp.einsum('bhcd,bhdv->bhcv', kcd_n, state)
        o_n = jnp.einsum('bhcd,bhdv->bhcv', qd_n, state) + jnp.einsum('bhcs,bhsv->bhcv', attn_n, v_new)
        state = state * cd_n[..., None, None] + jnp.einsum('bhcd,bhcv->bhdv', kt_n, v_new)
        return state, o_n

    xs = tuple(jnp.moveaxis(a, 2, 0) for a in (value, k_cumdecay, attn_intra, q_dec, k_tail, chunk_decay))
    state0 = jnp.zeros((bn, heads, dk, dv), jnp.float32)
    _, o = lax.scan(step, state0, xs)
    o = o.transpose(1, 0, 3, 2, 4).reshape(bn, total, heads, dv)
    return o[:, META_PAD:META_PAD + seq_len]


def hybrid_mixer(h, w_in, fgt_bias, gdn_conv_w, gdn_a_log, gdn_dt_bias, gdn_norm_w, gate_bias,
                 w_branch_fox, w_branch_gdn, w_out):
    bn, seq_len, _ = h.shape
    proj = h @ w_in.astype(h.dtype)
    fox_qkv, fox_f, gdn_qkv, gdn_z, gdn_b, gdn_a, gates = split_cols(proj, IN_SPLITS)
    fox_qkv = fox_qkv.reshape(bn, seq_len, 3, FOX_HEADS, FOX_HEAD_DIM)
    log_f = jax.nn.log_sigmoid(fox_f.astype(jnp.float32) + fgt_bias.astype(jnp.float32))
    o_fox = forgetting_attention(fox_qkv[:, :, 0], fox_qkv[:, :, 1], fox_qkv[:, :, 2], log_f)
    gdn_qkv = jax.nn.silu(causal_dwconv(gdn_qkv, gdn_conv_w))
    gdn_qkv = gdn_qkv.reshape(bn, seq_len, 3, GDN_HEADS, GDN_HEAD_DIM)
    beta = jax.nn.sigmoid(gdn_b.astype(jnp.float32))
    g = -jnp.exp(gdn_a_log.astype(jnp.float32)) * jax.nn.softplus(
        gdn_a.astype(jnp.float32) + gdn_dt_bias.astype(jnp.float32))
    o_gdn = gated_delta_rule(gdn_qkv[:, :, 0], gdn_qkv[:, :, 1], gdn_qkv[:, :, 2], beta, g)
    z = gdn_z.reshape(bn, seq_len, GDN_HEADS, GDN_HEAD_DIM).astype(jnp.float32)
    o_gdn = (rms_norm(o_gdn, gdn_norm_w) * jax.nn.silu(z)).astype(h.dtype).reshape(bn, seq_len, GDN_WIDTH)
    y_fox = o_fox @ w_branch_fox.astype(h.dtype)
    y_gdn = o_gdn @ w_branch_gdn.astype(h.dtype)
    gate = jax.nn.sigmoid(gates.astype(jnp.float32) + gate_bias.astype(jnp.float32)).astype(h.dtype)
    g_fox, g_gdn = gate[..., :D_MODEL], gate[..., D_MODEL:]
    return (g_fox * y_fox + g_gdn * y_gdn) @ w_out.astype(h.dtype)


def conv_ffn(h, w_up, conv_w, conv_b, w_down):
    u = causal_dwconv(h @ w_up.astype(h.dtype), conv_w, conv_b)
    gate, up = u[..., :D_FF], u[..., D_FF:]
    return (jax.nn.silu(gate) * up) @ w_down.astype(h.dtype)


def setup_inputs(seed: int = 0) -> dict:
    key = jax.random.key(seed)
    ks = jax.random.split(key, 20)
    f32 = jnp.float32

    def nrm(k, shape, scale):
        return jax.random.normal(k, shape, f32) * scale

    dt = jnp.exp(jax.random.uniform(ks[5], (DEPTH, GDN_HEADS), f32, np.log(1e-3), np.log(1e-1)))
    return {
        "x": nrm(ks[0], (BATCH, SEQ, D_MODEL), 1.0),
        "meta_tokens": nrm(ks[1], (N_META, D_MODEL), 1.0),
        "w_in": nrm(ks[2], (DEPTH, D_MODEL, IN_WIDTH), D_MODEL ** -0.5),
        "fgt_bias": 2.0 + nrm(ks[3], (DEPTH, FOX_HEADS), 0.5),
        "gdn_conv_w": nrm(ks[4], (DEPTH, GDN_CONV, 3 * GDN_WIDTH), GDN_CONV ** -0.5),
        "gdn_a_log": jnp.log(jax.random.uniform(ks[6], (DEPTH, GDN_HEADS), f32, 1.0, 16.0)),
        "gdn_dt_bias": dt + jnp.log(-jnp.expm1(-dt)),
        "gdn_norm_w": 1.0 + nrm(ks[7], (DEPTH, GDN_HEAD_DIM), 0.01),
        "gate_bias": nrm(ks[8], (DEPTH, 2 * D_MODEL), 0.01),
        "w_branch_fox": nrm(ks[9], (DEPTH, FOX_WIDTH, D_MODEL), FOX_WIDTH ** -0.5),
        "w_branch_gdn": nrm(ks[10], (DEPTH, GDN_WIDTH, D_MODEL), GDN_WIDTH ** -0.5),
        "w_out": nrm(ks[11], (DEPTH, D_MODEL, D_MODEL), D_MODEL ** -0.5),
        "norm_mix_w": 1.0 + nrm(ks[12], (DEPTH, D_MODEL), 0.01),
        "norm_ffn_w": 1.0 + nrm(ks[13], (DEPTH, D_MODEL), 0.01),
        "ffn_w_up": nrm(ks[14], (DEPTH, D_MODEL, 2 * D_FF), D_MODEL ** -0.5),
        "ffn_conv_w": nrm(ks[15], (DEPTH, FFN_CONV, 2 * D_FF), FFN_CONV ** -0.5),
        "ffn_conv_b": nrm(ks[16], (DEPTH, 2 * D_FF), 0.01),
        "ffn_w_down": nrm(ks[17], (DEPTH, D_FF, D_MODEL), D_FF ** -0.5),
        "norm_final_w": 1.0 + nrm(ks[18], (D_MODEL,), 0.01),
    }


def reference(x, meta_tokens, w_in, fgt_bias, gdn_conv_w, gdn_a_log, gdn_dt_bias, gdn_norm_w,
              gate_bias, w_branch_fox, w_branch_gdn, w_out, norm_mix_w, norm_ffn_w, ffn_w_up,
              ffn_conv_w, ffn_conv_b, ffn_w_down, norm_final_w):
    bn = x.shape[0]
    meta = jnp.broadcast_to(meta_tokens[None].astype(x.dtype), (bn, N_META, D_MODEL))
    h = jnp.concatenate([meta, x], axis=1)
    for layer in range(DEPTH):
        h = h + hybrid_mixer(rms_norm(h, norm_mix_w[layer]), w_in[layer], fgt_bias[layer],
                             gdn_conv_w[layer], gdn_a_log[layer], gdn_dt_bias[layer],
                             gdn_norm_w[layer], gate_bias[layer], w_branch_fox[layer],
                             w_branch_gdn[layer], w_out[layer])
        h = h + conv_ffn(rms_norm(h, norm_ffn_w[layer]), ffn_w_up[layer], ffn_conv_w[layer],
                         ffn_conv_b[layer], ffn_w_down[layer])
    return rms_norm(h, norm_final_w)[:, N_META:]
```

```python
import functools
import math

import jax
import jax.numpy as jnp
import numpy as np
from jax import lax
from jax.experimental import pallas as pl
from jax.experimental.pallas import tpu as pltpu

F32 = jnp.float32
BF16 = jnp.bfloat16

D_MODEL = 1024
N_META = 16
HEADS = 8
HEAD_DIM = 64
WIDTH = HEADS * HEAD_DIM
GDN_CONV = 4
D_FF = 2816
FFN_CONV = 3
CHUNK = 64
RMS_EPS = 1e-6
NEG = -1e30
LOG2E = math.log2(math.e)

LANES = 128
FRONT = 256
PAD_ROWS = FRONT - N_META
TM = 256
TQ = 256
TK = 128
GROWS = 2 * CHUNK
FF_CHUNK = 256
VMEM_LIMIT = 56 * 1024 * 1024

L_F = 0
L_B = 24
L_A = 32
A_QC = 64
A_KC = 67
A_PAD = 70


def _const_spec(shape):
    n = len(shape)
    return pl.BlockSpec(shape, lambda *_: (0,) * n, pipeline_mode=pl.Buffered(1))


def _split3(x):
    hi = x.astype(BF16)
    r = x - hi.astype(F32)
    mid = r.astype(BF16)
    lo = (r - mid.astype(F32)).astype(BF16)
    return hi, mid, lo


def _dot(a, b):
    return jnp.dot(a, b, preferred_element_type=F32)


def _dot_split_lhs(x, m, n):
    parts = _split3(x)[:n]
    out = _dot(parts[0], m)
    for p in parts[1:]:
        out = out + _dot(p, m)
    return out


def _dot_split_rhs(m, x, n):
    parts = _split3(x)[:n]
    out = _dot(m, parts[0])
    for p in parts[1:]:
        out = out + _dot(m, p)
    return out


def _rms(h, w):
    ms = jnp.mean(h * h, axis=-1, keepdims=True)
    return h * lax.rsqrt(ms + RMS_EPS) * w


def _log_sigmoid(x):
    return jnp.minimum(x, 0.0) - jnp.log1p(jnp.exp(-jnp.abs(x)))


def _softplus(x):
    return jnp.maximum(x, 0.0) + jnp.log1p(jnp.exp(-jnp.abs(x)))


def _silu(x):
    return x * jax.nn.sigmoid(x)


def _inproj_kernel(x_ref, meta_ref, nw_ref, wq_ref, wk_ref, wv_ref, ws_ref, wg_ref, wz_ref,
                   wgate_ref, sbias_ref, placeq_ref, placek_ref, onesq_ref, onesk_ref, tril_ref,
                   qT_ref, kaug_ref, vT_ref, gqkv_ref, z_ref, small_ref, gates_ref, carry_ref):
    i = pl.program_id(1)
    h = jnp.where(i == 0, meta_ref[...], x_ref[0])
    hn = _rms(h, nw_ref[...]).astype(BF16)

    small = _dot(hn, ws_ref[...])
    small_ref[0] = small

    @pl.when(i == 0)
    def _():
        carry_ref[...] = jnp.zeros_like(carry_ref)

    lf = _log_sigmoid(small + sbias_ref[...]) * LOG2E
    c = _dot_split_rhs(tril_ref[...], lf, 3) + carry_ref[0:1, :]
    carry_ref[...] = jnp.broadcast_to(c[TM - 1:TM, :], carry_ref.shape)
    hi = c.astype(BF16).astype(F32)
    r = c - hi
    mid = r.astype(BF16).astype(F32)
    lo = r - mid
    lane = lax.broadcasted_iota(jnp.int32, c.shape, 1)
    parts = jnp.where(lane < HEADS, hi, jnp.where(lane < 2 * HEADS, mid, lo)).astype(BF16)

    kaug = _dot(hn, wk_ref[...]) + _dot(parts, placek_ref[...]) + onesk_ref[...]
    row = lax.broadcasted_iota(jnp.int32, kaug.shape, 0)
    col = lax.broadcasted_iota(jnp.int32, kaug.shape, 1)
    inert = jnp.logical_and(jnp.logical_and(i == 0, row < PAD_ROWS), (col % LANES) == A_PAD)
    kaug = jnp.where(inert, NEG, kaug).astype(BF16)
    for hd in range(HEADS):
        kaug_ref[0, hd] = kaug[:, hd * LANES:(hd + 1) * LANES]

    qaug = (_dot(hn, wq_ref[...]) * (HEAD_DIM ** -0.5 * LOG2E)
            + _dot(parts, placeq_ref[...]) + onesq_ref[...])
    qT_ref[0, :, 0] = qaug.T.reshape(HEADS, LANES, TM).astype(BF16)

    vT = _dot(hn, wv_ref[...]).T
    for s in range(TM // TK):
        vT_ref[0, :, s] = vT[:, s * TK:(s + 1) * TK].reshape(HEADS, HEAD_DIM, TK).astype(BF16)

    gqkv_ref[0] = _dot(hn, wg_ref[...])
    z_ref[0] = _dot(hn, wz_ref[...])
    gates_ref[0] = _dot(hn, wgate_ref[...])


def _fox_kernel(qT_ref, k_ref, vT_ref, o_ref):
    i = pl.program_id(2)
    qTs = [qT_ref[0, hh, 0] for hh in range(2)]

    def step(j, carry, diag):
        new = []
        for hh in range(2):
            m, l, acc = carry[hh]
            k = k_ref[0, hh, pl.ds(pl.multiple_of(j * TK, TK), TK), :]
            sT = _dot(k, qTs[hh])
            if diag is not None:
                rows = lax.broadcasted_iota(jnp.int32, sT.shape, 0) + diag * TK
                cols = lax.broadcasted_iota(jnp.int32, sT.shape, 1)
                sT = jnp.where(rows <= cols, sT, NEG)
            m_new = jnp.maximum(m, jnp.max(sT, axis=0, keepdims=True))
            alpha = jnp.exp2(m - m_new)
            p = jnp.exp2(sT - m_new)
            l = alpha * l + jnp.sum(p, axis=0, keepdims=True)
            acc = alpha * acc + _dot(vT_ref[0, hh, j], p.astype(BF16))
            new.append((m_new, l, acc))
        return tuple(new)

    init = tuple((jnp.full((1, TQ), NEG, F32), jnp.zeros((1, TQ), F32),
                  jnp.zeros((HEAD_DIM, TQ), F32)) for _ in range(2))
    nfull = i * (TQ // TK)
    carry = lax.fori_loop(0, nfull, lambda j, c: step(j, c, None), init)
    for d in range(TQ // TK):
        carry = step(nfull + d, carry, d)
    oT = jnp.concatenate([acc / l for (_, l, acc) in carry], axis=0)
    o_ref[0] = oT.T.astype(o_ref.dtype)


def _nt(a, b):
    return lax.dot_general(a, b, (((1,), (1,)), ((), ())), preferred_element_type=F32)


def _tn(a, b):
    return lax.dot_general(a, b, (((0,), (0,)), ((), ())), preferred_element_type=F32)


def _unit_lower_inverse(a):
    n = a.shape[0]
    eye = (lax.broadcasted_iota(jnp.int32, (n, n), 0)
           == lax.broadcasted_iota(jnp.int32, (n, n), 1)).astype(F32)
    t = eye - a
    p = _dot(a, a)
    steps = int(math.log2(n)) - 1
    for s in range(steps):
        t = t + _dot(t, p)
        if s + 1 < steps:
            p = _dot(p, p)
    return t


def _gdn_kernel(gq_ref, small_ref, z_ref, convw_ref, gbias_ref, alog_ref, normw_ref,
                tril_ref, last_ref, eb_ref, eg_ref, bd_ref, bdmask_ref,
                o_ref, convbuf, state, obuf):
    i = pl.program_id(1)

    @pl.when(i == 0)
    def _():
        convbuf[0:8, :] = jnp.zeros((8, 3 * WIDTH), F32)
        state[...] = jnp.zeros_like(state)

    convbuf[8:8 + GROWS, :] = gq_ref[0]
    w = convw_ref[...]
    xc = w[GDN_CONV - 1:GDN_CONV, :] * convbuf[8:8 + GROWS, :]
    for t in range(1, GDN_CONV):
        xc = xc + w[GDN_CONV - 1 - t:GDN_CONV - t, :] * convbuf[8 - t:8 - t + GROWS, :]
    convbuf[0:8, :] = convbuf[GROWS:GROWS + 8, :]
    xs = _silu(xc)
    q = xs[:, :WIDTH]
    k = xs[:, WIDTH:2 * WIDTH]
    v = xs[:, 2 * WIDTH:]
    bd = bd_ref[...]
    qn = q * lax.rsqrt(_dot_split_lhs(q * q, bd, 2) + RMS_EPS) * (HEAD_DIM ** -0.5)
    kn = k * lax.rsqrt(_dot_split_lhs(k * k, bd, 2) + RMS_EPS)

    small = small_ref[0]
    beta_e = _dot_split_lhs(jax.nn.sigmoid(small), eb_ref[...], 2)
    g = -jnp.exp(alog_ref[...]) * _softplus(small + gbias_ref[...])
    gc = _dot_split_rhs(tril_ref[...], g, 3)
    gl = _dot_split_rhs(last_ref[...], gc, 3)
    gc_e = _dot_split_lhs(gc, eg_ref[...], 3)
    gl_e = _dot_split_lhs(gl, eg_ref[...], 3)
    gcT = gc.T
    eg = jnp.exp(gc_e)
    qd = qn * eg
    kb = kn * beta_e
    kbe = kb * eg
    kt = kn * jnp.exp(gl_e - gc_e)
    vb = v * beta_e
    cd_e = jnp.exp(gl_e)

    rr = lax.broadcasted_iota(jnp.int32, (CHUNK, CHUNK), 0)
    cc = lax.broadcasted_iota(jnp.int32, (CHUNK, CHUNK), 1)
    tril = rr >= cc
    strict = rr > cc
    lane = lax.broadcasted_iota(jnp.int32, (1, 2 * HEAD_DIM), 1)
    hmask = [lane < HEAD_DIM, lane >= HEAD_DIM]
    bdmask = bdmask_ref[...]

    for c in range(GROWS // CHUNK):
        r0 = c * CHUNK
        rows = slice(r0, r0 + CHUNK)
        for p in range(HEADS // 2):
            cols = slice(p * 2 * HEAD_DIM, (p + 1) * 2 * HEAD_DIM)
            kn_p, qn_p, kb_p = kn[rows, cols], qn[rows, cols], kb[rows, cols]
            vb_p, kbe_p = vb[rows, cols], kbe[rows, cols]
            s_prev = state[p]
            sol = None
            attns = []
            for hh in range(2):
                hd = 2 * p + hh
                gcol = gc[rows, L_A + hd:L_A + hd + 1]
                grow = gcT[L_A + hd:L_A + hd + 1, rows]
                decay = jnp.where(tril, jnp.exp(jnp.where(tril, gcol - grow, 0.0)), 0.0)
                a = jnp.where(strict, _nt(jnp.where(hmask[hh], kb_p, 0.0), kn_p) * decay, 0.0)
                attns.append(_nt(jnp.where(hmask[hh], qn_p, 0.0), kn_p) * decay)
                t = _unit_lower_inverse(a)
                rhs = jnp.concatenate([jnp.where(hmask[hh], vb_p, 0.0),
                                       jnp.where(hmask[hh], kbe_p, 0.0)], axis=1)
                s_h = _dot(t, rhs)
                sol = s_h if sol is None else sol + s_h
            value = sol[:, :2 * HEAD_DIM]
            kcd = sol[:, 2 * HEAD_DIM:]
            v_new = value - _dot(kcd, s_prev)
            o = _dot(qd[rows, cols], s_prev)
            for hh in range(2):
                o = o + _dot(attns[hh], jnp.where(hmask[hh], v_new, 0.0))
            cd = cd_e[r0 + CHUNK - 1:r0 + CHUNK, cols]
            state[p] = (s_prev * cd + _tn(kt[rows, cols], v_new)) * bdmask
            obuf[rows, cols] = o

    o = obuf[...]
    ms = _dot_split_lhs(o * o, bd, 2) * (1.0 / HEAD_DIM)
    o_ref[0] = (o * lax.rsqrt(ms + RMS_EPS) * normw_ref[...] * _silu(z_ref[0])).astype(o_ref.dtype)


def _merge_kernel(x_ref, meta_ref, ofox_ref, ogdn_ref, gates_ref, gbias_ref, wbf_ref, wbg_ref,
                  wout_ref, h2_ref):
    i = pl.program_id(1)
    h = jnp.where(i == 0, meta_ref[...], x_ref[0])
    y_fox = _dot(ofox_ref[0].astype(BF16), wbf_ref[...])
    y_gdn = _dot(ogdn_ref[0].astype(BF16), wbg_ref[...])
    gate = jax.nn.sigmoid(gates_ref[0] + gbias_ref[...])
    mix = gate[:, :D_MODEL] * y_fox + gate[:, D_MODEL:] * y_gdn
    h2 = h + _dot(mix.astype(BF16), wout_ref[...])
    row = lax.broadcasted_iota(jnp.int32, h2.shape, 0)
    h2_ref[0] = jnp.where(jnp.logical_and(i == 0, row < PAD_ROWS), 0.0, h2)


def _ffn_kernel(h2_ref, nw_ref, wup_ref, convw_ref, convb_ref, wdown_ref, fw_ref, o_ref,
                ubuf, tail, act):
    i = pl.program_id(1)

    @pl.when(i == 0)
    def _():
        tail[...] = jnp.zeros_like(tail)

    h2 = h2_ref[0]
    hn = _rms(h2, nw_ref[...]).astype(BF16)

    def conv_cols(c0, width):
        ubuf[0:8, 0:width] = tail[:, c0:c0 + width]
        ubuf[8:8 + TM, 0:width] = _dot(hn, wup_ref[:, c0:c0 + width])
        tail[:, c0:c0 + width] = ubuf[TM:TM + 8, 0:width]
        w = convw_ref[:, c0:c0 + width]
        y = convb_ref[:, c0:c0 + width] + w[FFN_CONV - 1:FFN_CONV, :] * ubuf[8:8 + TM, 0:width]
        for t in range(1, FFN_CONV):
            y = y + w[FFN_CONV - 1 - t:FFN_CONV - t, :] * ubuf[8 - t:8 - t + TM, 0:width]
        return y

    for n in range(D_FF // FF_CHUNK):
        c0 = n * FF_CHUNK
        gate = conv_cols(c0, FF_CHUNK)
        up = conv_cols(D_FF + c0, FF_CHUNK)
        act[:, c0:c0 + FF_CHUNK] = (_silu(gate) * up).astype(BF16)

    h3 = h2 + _dot(act[...], wdown_ref[...])
    o_ref[0] = _rms(h3, fw_ref[...])


def _constants():
    placeq = np.zeros((LANES, HEADS * LANES), np.float32)
    placek = np.zeros((LANES, HEADS * LANES), np.float32)
    onesq = np.zeros((1, HEADS * LANES), np.float32)
    onesk = np.zeros((1, HEADS * LANES), np.float32)
    for hd in range(HEADS):
        for part in range(3):
            placeq[part * HEADS + hd, hd * LANES + A_QC + part] = 1.0
            placek[part * HEADS + hd, hd * LANES + A_KC + part] = -1.0
        onesq[0, hd * LANES + A_KC:hd * LANES + A_PAD + 1] = 1.0
        onesk[0, hd * LANES + A_QC:hd * LANES + A_QC + 3] = 1.0
    tril_tm = np.tril(np.ones((TM, TM), np.float32))
    r = np.arange(GROWS)
    same = (r[:, None] // CHUNK) == (r[None, :] // CHUNK)
    tril_g = (same & (r[:, None] >= r[None, :])).astype(np.float32)
    last_g = (r[None, :] == (r[:, None] // CHUNK) * CHUNK + CHUNK - 1).astype(np.float32)
    eb = np.zeros((LANES, WIDTH), np.float32)
    eg = np.zeros((LANES, WIDTH), np.float32)
    for hd in range(HEADS):
        eb[L_B + hd, hd * HEAD_DIM:(hd + 1) * HEAD_DIM] = 1.0
        eg[L_A + hd, hd * HEAD_DIM:(hd + 1) * HEAD_DIM] = 1.0
    cw = np.arange(WIDTH)
    bd = ((cw[:, None] // HEAD_DIM) == (cw[None, :] // HEAD_DIM)).astype(np.float32)
    c2 = np.arange(2 * HEAD_DIM)
    bdmask = ((c2[:, None] // HEAD_DIM) == (c2[None, :] // HEAD_DIM)).astype(np.float32)
    return dict(placeq=placeq, placek=placek, onesq=onesq, onesk=onesk, tril_tm=tril_tm,
                tril_g=tril_g, last_g=last_g, eb=eb, eg=eg, bd=bd, bdmask=bdmask)


def _spread_heads(w):
    d = w.shape[0]
    w = w.reshape(d, HEADS, HEAD_DIM)
    return jnp.pad(w, ((0, 0), (0, 0), (0, LANES - HEAD_DIM))).reshape(d, HEADS * LANES)


def _lane_row(vals, start, copies=1):
    out = jnp.zeros((1, LANES), F32)
    for c in range(copies):
        out = lax.dynamic_update_slice(out, vals.reshape(1, -1).astype(F32), (0, start + c * vals.shape[-1]))
    return out


def kernel(x, meta_tokens, w_in, fgt_bias, gdn_conv_w, gdn_a_log, gdn_dt_bias, gdn_norm_w, gate_bias,
           w_branch_fox, w_branch_gdn, w_out, norm_mix_w, norm_ffn_w, ffn_w_up, ffn_conv_w,
           ffn_conv_b, ffn_w_down, norm_final_w):
    bn, seq, d = x.shape
    assert d == D_MODEL and seq % TM == 0 and w_in.shape[0] == 1
    p_rows = seq + FRONT
    nt = p_rows // TM
    cst = _constants()

    w = w_in[0]
    o = 0
    wq, wk, wv = (w[:, o + j * WIDTH:o + (j + 1) * WIDTH] for j in range(3))
    o += 3 * WIDTH
    wf = w[:, o:o + HEADS]
    o += HEADS
    wg = w[:, o:o + 3 * WIDTH]
    o += 3 * WIDTH
    wz = w[:, o:o + WIDTH]
    o += WIDTH
    wb = w[:, o:o + HEADS]
    o += HEADS
    wa = w[:, o:o + HEADS]
    o += HEADS
    wgate = w[:, o:]
    ws = jnp.concatenate([wf, wf, wf, wb, wa, jnp.zeros((d, LANES - 5 * HEADS), F32)], axis=1)
    meta_tile = jnp.concatenate([jnp.zeros((PAD_ROWS, d), F32), meta_tokens.astype(F32)], axis=0)
    sbias = _lane_row(fgt_bias[0], L_F, copies=3)
    gbias = _lane_row(gdn_dt_bias[0], L_A)
    alog = _lane_row(gdn_a_log[0], L_A)
    row = lambda a: a.reshape(1, -1).astype(F32)
    b16 = lambda a: jnp.asarray(a).astype(BF16)

    cparams = lambda sem: pltpu.CompilerParams(dimension_semantics=sem, vmem_limit_bytes=VMEM_LIMIT)
    x_spec = pl.BlockSpec((1, TM, d), lambda b, i: (b, jnp.maximum(i - 1, 0), 0))
    tile = lambda width: pl.BlockSpec((1, TM, width), lambda b, i: (b, i, 0))

    k1_in = [x, meta_tile, row(norm_mix_w[0]), b16(_spread_heads(wq)), b16(_spread_heads(wk)), b16(wv),
             b16(ws), b16(wg), b16(wz), b16(wgate), sbias, b16(cst["placeq"]), b16(cst["placek"]),
             jnp.asarray(cst["onesq"]), jnp.asarray(cst["onesk"]), b16(cst["tril_tm"])]
    k1_specs = [x_spec] + [_const_spec(a.shape) for a in k1_in[1:]]
    qT, kaug, vT, gqkv, z, small, gates = pl.pallas_call(
        _inproj_kernel,
        grid=(bn, nt),
        in_specs=k1_specs,
        out_specs=[
            pl.BlockSpec((1, HEADS, 1, LANES, TM), lambda b, i: (b, 0, i, 0, 0)),
            pl.BlockSpec((1, HEADS, TM, LANES), lambda b, i: (b, 0, i, 0)),
            pl.BlockSpec((1, HEADS, TM // TK, HEAD_DIM, TK), lambda b, i: (b, 0, i, 0, 0)),
            tile(3 * WIDTH), tile(WIDTH), tile(LANES), tile(2 * D_MODEL)],
        out_shape=[
            jax.ShapeDtypeStruct((bn, HEADS, nt, LANES, TM), BF16),
            jax.ShapeDtypeStruct((bn, HEADS, p_rows, LANES), BF16),
            jax.ShapeDtypeStruct((bn, HEADS, p_rows // TK, HEAD_DIM, TK), BF16),
            jax.ShapeDtypeStruct((bn, p_rows, 3 * WIDTH), F32),
            jax.ShapeDtypeStruct((bn, p_rows, WIDTH), F32),
            jax.ShapeDtypeStruct((bn, p_rows, LANES), F32),
            jax.ShapeDtypeStruct((bn, p_rows, 2 * D_MODEL), F32)],
        scratch_shapes=[pltpu.VMEM((8, LANES), F32)],
        compiler_params=cparams(("arbitrary", "arbitrary")),
        name="inproj",
    )(*k1_in)

    nq = p_rows // TQ
    o_fox = pl.pallas_call(
        _fox_kernel,
        grid=(bn, HEADS // 2, nq),
        in_specs=[
            pl.BlockSpec((1, 2, 1, LANES, TQ), lambda b, p, i: (b, p, i, 0, 0)),
            pl.BlockSpec((1, 2, p_rows, LANES), lambda b, p, i: (b, p, 0, 0)),
            pl.BlockSpec((1, 2, p_rows // TK, HEAD_DIM, TK), lambda b, p, i: (b, p, 0, 0, 0))],
        out_specs=pl.BlockSpec((1, TQ, 2 * HEAD_DIM), lambda b, p, i: (b, i, p)),
        out_shape=jax.ShapeDtypeStruct((bn, p_rows, WIDTH), BF16),
        compiler_params=cparams(("arbitrary", "arbitrary", "arbitrary")),
        name="fox_attention",
    )(qT, kaug, vT)

    gtile = lambda width: pl.BlockSpec((1, GROWS, width), lambda b, i: (b, i, 0))
    k3_const = [gdn_conv_w[0].astype(F32), gbias, alog, row(jnp.tile(gdn_norm_w[0], HEADS)),
                b16(cst["tril_g"]), b16(cst["last_g"]), b16(cst["eb"]), b16(cst["eg"]), b16(cst["bd"]),
                jnp.asarray(cst["bdmask"])]
    o_gdn = pl.pallas_call(
        _gdn_kernel,
        grid=(bn, p_rows // GROWS),
        in_specs=[gtile(3 * WIDTH), gtile(LANES), gtile(WIDTH)] + [_const_spec(a.shape) for a in k3_const],
        out_specs=gtile(WIDTH),
        out_shape=jax.ShapeDtypeStruct((bn, p_rows, WIDTH), BF16),
        scratch_shapes=[pltpu.VMEM((GROWS + 8, 3 * WIDTH), F32),
                        pltpu.VMEM((HEADS // 2, 2 * HEAD_DIM, 2 * HEAD_DIM), F32),
                        pltpu.VMEM((GROWS, WIDTH), F32)],
        compiler_params=cparams(("arbitrary", "arbitrary")),
        name="gated_deltanet",
    )(gqkv, small, z, *k3_const)

    k4_const = [row(gate_bias[0]), b16(w_branch_fox[0]), b16(w_branch_gdn[0]), b16(w_out[0])]
    h2 = pl.pallas_call(
        _merge_kernel,
        grid=(bn, nt),
        in_specs=[x_spec, _const_spec(meta_tile.shape), tile(WIDTH), tile(WIDTH), tile(2 * D_MODEL)]
        + [_const_spec(a.shape) for a in k4_const],
        out_specs=tile(d),
        out_shape=jax.ShapeDtypeStruct((bn, p_rows, d), F32),
        compiler_params=cparams(("arbitrary", "arbitrary")),
        name="merge",
    )(x, meta_tile, o_fox, o_gdn, gates, *k4_const)

    k5_const = [row(norm_ffn_w[0]), b16(ffn_w_up[0]), ffn_conv_w[0].astype(F32), row(ffn_conv_b[0]),
                b16(ffn_w_down[0]), row(norm_final_w)]
    out = pl.pallas_call(
        _ffn_kernel,
        grid=(bn, nt),
        in_specs=[tile(d)] + [_const_spec(a.shape) for a in k5_const],
        out_specs=pl.BlockSpec((1, TM, d), lambda b, i: (b, jnp.maximum(i - 1, 0), 0)),
        out_shape=jax.ShapeDtypeStruct((bn, seq, d), F32),
        scratch_shapes=[pltpu.VMEM((TM + 8, FF_CHUNK), F32),
                        pltpu.VMEM((8, 2 * D_FF), F32),
                        pltpu.VMEM((TM, D_FF), BF16)],
        compiler_params=cparams(("arbitrary", "arbitrary")),
        name="conv_ffn",
    )(h2, *k5_const)
    return out
```

```python
import functools
import math

import jax
import jax.numpy as jnp
import numpy as np
from jax import lax
from jax.experimental import pallas as pl
from jax.experimental.pallas import tpu as pltpu

F32 = jnp.float32
BF16 = jnp.bfloat16

D_MODEL = 1024
N_META = 16
HEADS = 8
HEAD_DIM = 64
WIDTH = HEADS * HEAD_DIM
GDN_CONV = 4
D_FF = 2816
FFN_CONV = 3
CHUNK = 64
RMS_EPS = 1e-6
NEG = -1e30
LOG2E = math.log2(math.e)

LANES = 128
FRONT = 512
PAD_ROWS = FRONT - N_META
TM = 256
NF = FRONT // TM
TQ = 512
TK = 512
TPA = TQ // TM
GROWS = 2 * CHUNK
FF_CHUNK = 256
VMEM_LIMIT = 56 * 1024 * 1024

L_F = 0
L_B = 24
L_A = 32
A_QC = 64
A_KC = 67
A_PAD = 70


def _const_spec(shape):
    n = len(shape)
    return pl.BlockSpec(shape, lambda *_: (0,) * n, pipeline_mode=pl.Buffered(1))


def _split3(x):
    hi = x.astype(BF16)
    r = x - hi.astype(F32)
    mid = r.astype(BF16)
    lo = (r - mid.astype(F32)).astype(BF16)
    return hi, mid, lo


def _dot(a, b):
    return jnp.dot(a, b, preferred_element_type=F32)


def _dot_split_lhs(x, m, n):
    parts = _split3(x)[:n]
    out = _dot(parts[0], m)
    for p in parts[1:]:
        out = out + _dot(p, m)
    return out


def _dot_split_rhs(m, x, n):
    parts = _split3(x)[:n]
    out = _dot(m, parts[0])
    for p in parts[1:]:
        out = out + _dot(m, p)
    return out


def _rms(h, w):
    ms = jnp.mean(h * h, axis=-1, keepdims=True)
    return h * lax.rsqrt(ms + RMS_EPS) * w


def _log_sigmoid(x):
    return jnp.minimum(x, 0.0) - jnp.log1p(jnp.exp(-jnp.abs(x)))


def _softplus(x):
    return jnp.maximum(x, 0.0) + jnp.log1p(jnp.exp(-jnp.abs(x)))


def _silu(x):
    return x * jax.nn.sigmoid(x)


def _inproj_kernel(x_ref, meta_ref, nw_ref, wq_ref, wk_ref, wv_ref, ws_ref, wg_ref, wz_ref,
                   wgate_ref, sbias_ref, placeq_ref, placek_ref, onesq_ref, onesk_ref, tril_ref,
                   qT_ref, kaug_ref, vT_ref, gqkv_ref, z_ref, small_ref, gates_ref, carry_ref):
    i = pl.program_id(1)
    h = jnp.where(i < NF, meta_ref[...], x_ref[0])
    hn = _rms(h, nw_ref[...]).astype(BF16)

    small = _dot(hn, ws_ref[...])
    small_ref[0] = small

    @pl.when(i == 0)
    def _():
        carry_ref[...] = jnp.zeros_like(carry_ref)

    lf = _log_sigmoid(small + sbias_ref[...]) * LOG2E
    c = _dot_split_rhs(tril_ref[...], lf, 3) + carry_ref[0:1, :]
    carry_ref[...] = jnp.broadcast_to(c[TM - 1:TM, :], carry_ref.shape)
    hi = c.astype(BF16).astype(F32)
    r = c - hi
    mid = r.astype(BF16).astype(F32)
    lo = r - mid
    lane = lax.broadcasted_iota(jnp.int32, c.shape, 1)
    parts = jnp.where(lane < HEADS, hi, jnp.where(lane < 2 * HEADS, mid, lo)).astype(BF16)

    kaug = _dot(hn, wk_ref[...]) + _dot(parts, placek_ref[...]) + onesk_ref[...]
    row = lax.broadcasted_iota(jnp.int32, kaug.shape, 0)
    col = lax.broadcasted_iota(jnp.int32, kaug.shape, 1)
    inert = jnp.logical_and(i * TM + row < PAD_ROWS, (col % LANES) == A_PAD)
    kaug = jnp.where(inert, NEG, kaug).astype(BF16)
    for hd in range(HEADS):
        kaug_ref[0, hd] = kaug[:, hd * LANES:(hd + 1) * LANES]

    qaug = (_dot(hn, wq_ref[...]) * (HEAD_DIM ** -0.5 * LOG2E)
            + _dot(parts, placeq_ref[...]) + onesq_ref[...])
    qT_ref[0, :, 0] = qaug.T.reshape(HEADS, LANES, TM).astype(BF16)

    vT = _dot(hn, wv_ref[...]).T
    vT_ref[0, :, 0] = vT.reshape(HEADS, HEAD_DIM, TM).astype(BF16)

    gqkv_ref[0] = _dot(hn, wg_ref[...])
    z_ref[0] = _dot(hn, wz_ref[...])
    gates_ref[0] = _dot(hn, wgate_ref[...])


def _fox_kernel(qT_ref, k_ref, vT_ref, o_ref):
    i = pl.program_id(2)
    qTs = [qT_ref[0, hh, 0] for hh in range(2)]

    def scores(j):
        off = pl.multiple_of(j * TK, TK)
        return tuple(_dot(k_ref[0, hh, pl.ds(off, TK), :], qTs[hh]) for hh in range(2))

    def update(sT, m, l):
        m_new = jnp.maximum(m, jnp.max(sT, axis=0, keepdims=True))
        alpha = jnp.exp2(m - m_new)
        p = jnp.exp2(sT - m_new)
        l = alpha * l + jnp.sum(p, axis=0, keepdims=True)
        return m_new, l, alpha, p.astype(BF16)

    def body(j, carry):
        s_cur, st = carry
        s_next = scores(j + 1)
        jp = jnp.maximum(j - 1, 0)
        new = []
        for hh in range(2):
            m, l, acc, p_prev, a_prev = st[hh]
            acc = a_prev * acc + _dot(vT_ref[0, hh, jp], p_prev)
            m, l, alpha, p = update(s_cur[hh], m, l)
            new.append((m, l, acc, p, alpha))
        return s_next, tuple(new)

    init = tuple((jnp.full((1, TQ), NEG, F32), jnp.zeros((1, TQ), F32),
                  jnp.zeros((HEAD_DIM, TQ), F32), jnp.zeros((TK, TQ), BF16),
                  jnp.ones((1, TQ), F32)) for _ in range(2))
    s_cur, st = lax.fori_loop(0, i, body, (scores(0), init))

    causal = (lax.broadcasted_iota(jnp.int32, (TK, TQ), 0)
              <= lax.broadcasted_iota(jnp.int32, (TK, TQ), 1))
    ip = jnp.maximum(i - 1, 0)
    outs = []
    for hh in range(2):
        m, l, acc, p_prev, a_prev = st[hh]
        acc = a_prev * acc + _dot(vT_ref[0, hh, ip], p_prev)
        m, l, alpha, p = update(jnp.where(causal, s_cur[hh], NEG), m, l)
        acc = alpha * acc + _dot(vT_ref[0, hh, i], p)
        outs.append(acc / l)
    oT = jnp.concatenate(outs, axis=0)
    o_ref[0] = oT.T.astype(o_ref.dtype)


def _nt(a, b):
    return lax.dot_general(a, b, (((1,), (1,)), ((), ())), preferred_element_type=F32)


def _tn(a, b):
    return lax.dot_general(a, b, (((0,), (0,)), ((), ())), preferred_element_type=F32)


def _gdn_kernel(gq_ref, small_ref, z_ref, convw_ref, gbias_ref, alog_ref, normw_ref,
                tril_ref, last_ref, eb_ref, eg_ref, bd_ref, bdmask_ref,
                o_ref, convbuf, state, obuf):
    i = pl.program_id(1)

    @pl.when(i == 0)
    def _():
        convbuf[0:8, :] = jnp.zeros((8, 3 * WIDTH), F32)
        state[...] = jnp.zeros_like(state)

    convbuf[8:8 + GROWS, :] = gq_ref[0]
    w = convw_ref[...]
    xc = w[GDN_CONV - 1:GDN_CONV, :] * convbuf[8:8 + GROWS, :]
    for t in range(1, GDN_CONV):
        xc = xc + w[GDN_CONV - 1 - t:GDN_CONV - t, :] * convbuf[8 - t:8 - t + GROWS, :]
    convbuf[0:8, :] = convbuf[GROWS:GROWS + 8, :]
    xs = _silu(xc)
    q = xs[:, :WIDTH]
    k = xs[:, WIDTH:2 * WIDTH]
    v = xs[:, 2 * WIDTH:]
    bd = bd_ref[...]
    qn = q * lax.rsqrt(_dot_split_lhs(q * q, bd, 2) + RMS_EPS) * (HEAD_DIM ** -0.5)
    kn = k * lax.rsqrt(_dot_split_lhs(k * k, bd, 2) + RMS_EPS)

    small = small_ref[0]
    beta_e = _dot_split_lhs(jax.nn.sigmoid(small), eb_ref[...], 2)
    g = -jnp.exp(alog_ref[...]) * _softplus(small + gbias_ref[...])
    gc = _dot_split_rhs(tril_ref[...], g, 3)
    gl = _dot_split_rhs(last_ref[...], gc, 3)
    gc_e = _dot_split_lhs(gc, eg_ref[...], 3)
    gl_e = _dot_split_lhs(gl, eg_ref[...], 3)
    gcT = gc.T
    eg = jnp.exp(gc_e)
    qd = qn * eg
    kb = kn * beta_e
    kbe = kb * eg
    kt = kn * jnp.exp(gl_e - gc_e)
    vb = v * beta_e
    cd_e = jnp.exp(gl_e)

    rr = lax.broadcasted_iota(jnp.int32, (CHUNK, CHUNK), 0)
    cc = lax.broadcasted_iota(jnp.int32, (CHUNK, CHUNK), 1)
    tril = rr >= cc
    strict = rr > cc
    lane = lax.broadcasted_iota(jnp.int32, (1, 2 * HEAD_DIM), 1)
    hmask = [lane < HEAD_DIM, lane >= HEAD_DIM]
    bdmask = bdmask_ref[...]

    nch = GROWS // CHUNK
    npair = HEADS // 2
    rows_of = lambda c: slice(c * CHUNK, (c + 1) * CHUNK)
    cols_of = lambda p: slice(p * 2 * HEAD_DIM, (p + 1) * 2 * HEAD_DIM)
    items = [(c, hd) for c in range(nch) for hd in range(HEADS)]
    a_l, attn_l = {}, {}
    for (c, hd) in items:
        p, hh = divmod(hd, 2)
        rows, cols = rows_of(c), cols_of(p)
        gcol = gc[rows, L_A + hd:L_A + hd + 1]
        grow = gcT[L_A + hd:L_A + hd + 1, rows]
        decay = jnp.where(tril, jnp.exp(jnp.where(tril, gcol - grow, 0.0)), 0.0)
        kn_p = kn[rows, cols]
        a_l[c, hd] = jnp.where(strict, _nt(jnp.where(hmask[hh], kb[rows, cols], 0.0), kn_p) * decay, 0.0)
        attn_l[c, hd] = _nt(jnp.where(hmask[hh], qn[rows, cols], 0.0), kn_p) * decay

    eye = (rr == cc).astype(F32)
    t_l = {it: eye - a_l[it] for it in items}
    p_l = {it: _dot(a_l[it], a_l[it]) for it in items}
    steps = int(math.log2(CHUNK)) - 1
    for s in range(steps):
        t_l = {it: t_l[it] + _dot(t_l[it], p_l[it]) for it in items}
        if s + 1 < steps:
            p_l = {it: _dot(p_l[it], p_l[it]) for it in items}

    sol_l = {}
    for (c, hd) in items:
        p, hh = divmod(hd, 2)
        rows, cols = rows_of(c), cols_of(p)
        rhs = jnp.concatenate([jnp.where(hmask[hh], vb[rows, cols], 0.0),
                               jnp.where(hmask[hh], kbe[rows, cols], 0.0)], axis=1)
        sol_l[c, hd] = _dot(t_l[c, hd], rhs)

    s_cur = [state[p] for p in range(npair)]
    for c in range(nch):
        rows = rows_of(c)
        sols = [sol_l[c, 2 * p] + sol_l[c, 2 * p + 1] for p in range(npair)]
        v_new = [sols[p][:, :2 * HEAD_DIM] - _dot(sols[p][:, 2 * HEAD_DIM:], s_cur[p])
                 for p in range(npair)]
        o_l = [_dot(qd[rows, cols_of(p)], s_cur[p]) for p in range(npair)]
        for hh in range(2):
            o_l = [o_l[p] + _dot(attn_l[c, 2 * p + hh], jnp.where(hmask[hh], v_new[p], 0.0))
                   for p in range(npair)]
        upd = [_tn(kt[rows, cols_of(p)], v_new[p]) for p in range(npair)]
        for p in range(npair):
            cd = cd_e[(c + 1) * CHUNK - 1:(c + 1) * CHUNK, cols_of(p)]
            s_cur[p] = (s_cur[p] * cd + upd[p]) * bdmask
            obuf[rows, cols_of(p)] = o_l[p]
    for p in range(npair):
        state[p] = s_cur[p]

    o = obuf[...]
    ms = _dot_split_lhs(o * o, bd, 2) * (1.0 / HEAD_DIM)
    o_ref[0] = (o * lax.rsqrt(ms + RMS_EPS) * normw_ref[...] * _silu(z_ref[0])).astype(o_ref.dtype)


def _merge_kernel(x_ref, meta_ref, ofox_ref, ogdn_ref, gates_ref, gbias_ref, wbf_ref, wbg_ref,
                  wout_ref, h2_ref):
    i = pl.program_id(1)
    h = jnp.where(i < NF, meta_ref[...], x_ref[0])
    y_fox = _dot(ofox_ref[0].astype(BF16), wbf_ref[...])
    y_gdn = _dot(ogdn_ref[0].astype(BF16), wbg_ref[...])
    gate = jax.nn.sigmoid(gates_ref[0] + gbias_ref[...])
    mix = gate[:, :D_MODEL] * y_fox + gate[:, D_MODEL:] * y_gdn
    h2 = h + _dot(mix.astype(BF16), wout_ref[...])
    row = lax.broadcasted_iota(jnp.int32, h2.shape, 0)
    h2_ref[0] = jnp.where(i * TM + row < PAD_ROWS, 0.0, h2)


def _ffn_kernel(h2_ref, nw_ref, wup_ref, convw_ref, convb_ref, wdown_ref, fw_ref, o_ref,
                ubuf, tail, act):
    i = pl.program_id(1)

    @pl.when(i == 0)
    def _():
        tail[...] = jnp.zeros_like(tail)

    h2 = h2_ref[0]
    hn = _rms(h2, nw_ref[...]).astype(BF16)

    def conv_cols(c0, width):
        ubuf[0:8, 0:width] = tail[:, c0:c0 + width]
        ubuf[8:8 + TM, 0:width] = _dot(hn, wup_ref[:, c0:c0 + width])
        tail[:, c0:c0 + width] = ubuf[TM:TM + 8, 0:width]
        w = convw_ref[:, c0:c0 + width]
        y = convb_ref[:, c0:c0 + width] + w[FFN_CONV - 1:FFN_CONV, :] * ubuf[8:8 + TM, 0:width]
        for t in range(1, FFN_CONV):
            y = y + w[FFN_CONV - 1 - t:FFN_CONV - t, :] * ubuf[8 - t:8 - t + TM, 0:width]
        return y

    for n in range(D_FF // FF_CHUNK):
        c0 = n * FF_CHUNK
        gate = conv_cols(c0, FF_CHUNK)
        up = conv_cols(D_FF + c0, FF_CHUNK)
        act[:, c0:c0 + FF_CHUNK] = (_silu(gate) * up).astype(BF16)

    h3 = h2 + _dot(act[...], wdown_ref[...])
    o_ref[0] = _rms(h3, fw_ref[...])


def _constants():
    placeq = np.zeros((LANES, HEADS * LANES), np.float32)
    placek = np.zeros((LANES, HEADS * LANES), np.float32)
    onesq = np.zeros((1, HEADS * LANES), np.float32)
    onesk = np.zeros((1, HEADS * LANES), np.float32)
    for hd in range(HEADS):
        for part in range(3):
            placeq[part * HEADS + hd, hd * LANES + A_QC + part] = 1.0
            placek[part * HEADS + hd, hd * LANES + A_KC + part] = -1.0
        onesq[0, hd * LANES + A_KC:hd * LANES + A_PAD + 1] = 1.0
        onesk[0, hd * LANES + A_QC:hd * LANES + A_QC + 3] = 1.0
    tril_tm = np.tril(np.ones((TM, TM), np.float32))
    r = np.arange(GROWS)
    same = (r[:, None] // CHUNK) == (r[None, :] // CHUNK)
    tril_g = (same & (r[:, None] >= r[None, :])).astype(np.float32)
    last_g = (r[None, :] == (r[:, None] // CHUNK) * CHUNK + CHUNK - 1).astype(np.float32)
    eb = np.zeros((LANES, WIDTH), np.float32)
    eg = np.zeros((LANES, WIDTH), np.float32)
    for hd in range(HEADS):
        eb[L_B + hd, hd * HEAD_DIM:(hd + 1) * HEAD_DIM] = 1.0
        eg[L_A + hd, hd * HEAD_DIM:(hd + 1) * HEAD_DIM] = 1.0
    cw = np.arange(WIDTH)
    bd = ((cw[:, None] // HEAD_DIM) == (cw[None, :] // HEAD_DIM)).astype(np.float32)
    c2 = np.arange(2 * HEAD_DIM)
    bdmask = ((c2[:, None] // HEAD_DIM) == (c2[None, :] // HEAD_DIM)).astype(np.float32)
    return dict(placeq=placeq, placek=placek, onesq=onesq, onesk=onesk, tril_tm=tril_tm,
                tril_g=tril_g, last_g=last_g, eb=eb, eg=eg, bd=bd, bdmask=bdmask)


def _spread_heads(w):
    d = w.shape[0]
    w = w.reshape(d, HEADS, HEAD_DIM)
    return jnp.pad(w, ((0, 0), (0, 0), (0, LANES - HEAD_DIM))).reshape(d, HEADS * LANES)


def _lane_row(vals, start, copies=1):
    out = jnp.zeros((1, LANES), F32)
    for c in range(copies):
        out = lax.dynamic_update_slice(out, vals.reshape(1, -1).astype(F32), (0, start + c * vals.shape[-1]))
    return out


def kernel(x, meta_tokens, w_in, fgt_bias, gdn_conv_w, gdn_a_log, gdn_dt_bias, gdn_norm_w, gate_bias,
           w_branch_fox, w_branch_gdn, w_out, norm_mix_w, norm_ffn_w, ffn_w_up, ffn_conv_w,
           ffn_conv_b, ffn_w_down, norm_final_w):
    bn, seq, d = x.shape
    assert d == D_MODEL and seq % TM == 0 and w_in.shape[0] == 1
    p_rows = seq + FRONT
    nt = p_rows // TM
    cst = _constants()

    w = w_in[0]
    o = 0
    wq, wk, wv = (w[:, o + j * WIDTH:o + (j + 1) * WIDTH] for j in range(3))
    o += 3 * WIDTH
    wf = w[:, o:o + HEADS]
    o += HEADS
    wg = w[:, o:o + 3 * WIDTH]
    o += 3 * WIDTH
    wz = w[:, o:o + WIDTH]
    o += WIDTH
    wb = w[:, o:o + HEADS]
    o += HEADS
    wa = w[:, o:o + HEADS]
    o += HEADS
    wgate = w[:, o:]
    ws = jnp.concatenate([wf, wf, wf, wb, wa, jnp.zeros((d, LANES - 5 * HEADS), F32)], axis=1)
    meta_tile = jnp.concatenate([jnp.zeros((PAD_ROWS, d), F32), meta_tokens.astype(F32)], axis=0)
    sbias = _lane_row(fgt_bias[0], L_F, copies=3)
    gbias = _lane_row(gdn_dt_bias[0], L_A)
    alog = _lane_row(gdn_a_log[0], L_A)
    row = lambda a: a.reshape(1, -1).astype(F32)
    b16 = lambda a: jnp.asarray(a).astype(BF16)

    cparams = lambda sem: pltpu.CompilerParams(dimension_semantics=sem, vmem_limit_bytes=VMEM_LIMIT)
    x_spec = pl.BlockSpec((1, TM, d), lambda b, i: (b, jnp.maximum(i - NF, 0), 0))
    meta_spec = pl.BlockSpec((TM, d), lambda b, i: (jnp.minimum(i, NF - 1), 0))
    tile = lambda width: pl.BlockSpec((1, TM, width), lambda b, i: (b, i, 0))

    k1_in = [x, meta_tile, row(norm_mix_w[0]), b16(_spread_heads(wq)), b16(_spread_heads(wk)), b16(wv),
             b16(ws), b16(wg), b16(wz), b16(wgate), sbias, b16(cst["placeq"]), b16(cst["placek"]),
             jnp.asarray(cst["onesq"]), jnp.asarray(cst["onesk"]), b16(cst["tril_tm"])]
    k1_specs = [x_spec, meta_spec] + [_const_spec(a.shape) for a in k1_in[2:]]
    qT, kaug, vT, gqkv, z, small, gates = pl.pallas_call(
        _inproj_kernel,
        grid=(bn, nt),
        in_specs=k1_specs,
        out_specs=[
            pl.BlockSpec((1, HEADS, 1, LANES, TM), lambda b, i: (b, 0, i // TPA, 0, i % TPA)),
            pl.BlockSpec((1, HEADS, TM, LANES), lambda b, i: (b, 0, i, 0)),
            pl.BlockSpec((1, HEADS, 1, HEAD_DIM, TM), lambda b, i: (b, 0, i // TPA, 0, i % TPA)),
            tile(3 * WIDTH), tile(WIDTH), tile(LANES), tile(2 * D_MODEL)],
        out_shape=[
            jax.ShapeDtypeStruct((bn, HEADS, p_rows // TQ, LANES, TQ), BF16),
            jax.ShapeDtypeStruct((bn, HEADS, p_rows, LANES), BF16),
            jax.ShapeDtypeStruct((bn, HEADS, p_rows // TK, HEAD_DIM, TK), BF16),
            jax.ShapeDtypeStruct((bn, p_rows, 3 * WIDTH), F32),
            jax.ShapeDtypeStruct((bn, p_rows, WIDTH), F32),
            jax.ShapeDtypeStruct((bn, p_rows, LANES), F32),
            jax.ShapeDtypeStruct((bn, p_rows, 2 * D_MODEL), F32)],
        scratch_shapes=[pltpu.VMEM((8, LANES), F32)],
        compiler_params=cparams(("arbitrary", "arbitrary")),
        name="inproj",
    )(*k1_in)

    nq = p_rows // TQ
    o_fox = pl.pallas_call(
        _fox_kernel,
        grid=(bn, HEADS // 2, nq),
        in_specs=[
            pl.BlockSpec((1, 2, 1, LANES, TQ), lambda b, p, i: (b, p, i, 0, 0)),
            pl.BlockSpec((1, 2, p_rows, LANES), lambda b, p, i: (b, p, 0, 0)),
            pl.BlockSpec((1, 2, p_rows // TK, HEAD_DIM, TK), lambda b, p, i: (b, p, 0, 0, 0))],
        out_specs=pl.BlockSpec((1, TQ, 2 * HEAD_DIM), lambda b, p, i: (b, i, p)),
        out_shape=jax.ShapeDtypeStruct((bn, p_rows, WIDTH), BF16),
        compiler_params=cparams(("arbitrary", "arbitrary", "arbitrary")),
        name="fox_attention",
    )(qT, kaug, vT)

    gtile = lambda width: pl.BlockSpec((1, GROWS, width), lambda b, i: (b, i, 0))
    k3_const = [gdn_conv_w[0].astype(F32), gbias, alog, row(jnp.tile(gdn_norm_w[0], HEADS)),
                b16(cst["tril_g"]), b16(cst["last_g"]), b16(cst["eb"]), b16(cst["eg"]), b16(cst["bd"]),
                jnp.asarray(cst["bdmask"])]
    o_gdn = pl.pallas_call(
        _gdn_kernel,
        grid=(bn, p_rows // GROWS),
        in_specs=[gtile(3 * WIDTH), gtile(LANES), gtile(WIDTH)] + [_const_spec(a.shape) for a in k3_const],
        out_specs=gtile(WIDTH),
        out_shape=jax.ShapeDtypeStruct((bn, p_rows, WIDTH), BF16),
        scratch_shapes=[pltpu.VMEM((GROWS + 8, 3 * WIDTH), F32),
                        pltpu.VMEM((HEADS // 2, 2 * HEAD_DIM, 2 * HEAD_DIM), F32),
                        pltpu.VMEM((GROWS, WIDTH), F32)],
        compiler_params=cparams(("arbitrary", "arbitrary")),
        name="gated_deltanet",
    )(gqkv, small, z, *k3_const)

    k4_const = [row(gate_bias[0]), b16(w_branch_fox[0]), b16(w_branch_gdn[0]), b16(w_out[0])]
    h2 = pl.pallas_call(
        _merge_kernel,
        grid=(bn, nt),
        in_specs=[x_spec, meta_spec, tile(WIDTH), tile(WIDTH), tile(2 * D_MODEL)]
        + [_const_spec(a.shape) for a in k4_const],
        out_specs=tile(d),
        out_shape=jax.ShapeDtypeStruct((bn, p_rows, d), F32),
        compiler_params=cparams(("arbitrary", "arbitrary")),
        name="merge",
    )(x, meta_tile, o_fox, o_gdn, gates, *k4_const)

    k5_const = [row(norm_ffn_w[0]), b16(ffn_w_up[0]), ffn_conv_w[0].astype(F32), row(ffn_conv_b[0]),
                b16(ffn_w_down[0]), row(norm_final_w)]
    out = pl.pallas_call(
        _ffn_kernel,
        grid=(bn, nt),
        in_specs=[tile(d)] + [_const_spec(a.shape) for a in k5_const],
        out_specs=pl.BlockSpec((1, TM, d), lambda b, i: (b, jnp.maximum(i - NF, 0), 0)),
        out_shape=jax.ShapeDtypeStruct((bn, seq, d), F32),
        scratch_shapes=[pltpu.VMEM((TM + 8, FF_CHUNK), F32),
                        pltpu.VMEM((8, 2 * D_FF), F32),
                        pltpu.VMEM((TM, D_FF), BF16)],
        compiler_params=cparams(("arbitrary", "arbitrary")),
        name="conv_ffn",
    )(h2, *k5_const)
    return out
```

```python
import functools
import math

import jax
import jax.numpy as jnp
import numpy as np
from jax import lax
from jax.experimental import pallas as pl
from jax.experimental.pallas import tpu as pltpu

F32 = jnp.float32
BF16 = jnp.bfloat16

D_MODEL = 1024
N_META = 16
HEADS = 8
HEAD_DIM = 64
WIDTH = HEADS * HEAD_DIM
GDN_CONV = 4
D_FF = 2816
FFN_CONV = 3
CHUNK = 64
RMS_EPS = 1e-6
NEG = -1e30
LOG2E = math.log2(math.e)

LANES = 128
FRONT = 512
PAD_ROWS = FRONT - N_META
TM = 256
NF = FRONT // TM
TQ = 512
TK = 512
TPA = TQ // TM
RC = 256
VROWS = HEAD_DIM + 16
GROWS = 4 * CHUNK
QH = 4
QW = QH * HEAD_DIM
FF_CHUNK = 256
VMEM_LIMIT = 56 * 1024 * 1024

L_F = 0
L_B = 24
L_A = 32
A_QC = 64
A_KC = 67
A_PAD = 70


def _const_spec(shape):
    n = len(shape)
    return pl.BlockSpec(shape, lambda *_: (0,) * n, pipeline_mode=pl.Buffered(1))


def _split3(x):
    hi = x.astype(BF16)
    r = x - hi.astype(F32)
    mid = r.astype(BF16)
    lo = (r - mid.astype(F32)).astype(BF16)
    return hi, mid, lo


def _dot(a, b):
    return jnp.dot(a, b, preferred_element_type=F32)


def _dot_split_lhs(x, m, n):
    parts = _split3(x)[:n]
    out = _dot(parts[0], m)
    for p in parts[1:]:
        out = out + _dot(p, m)
    return out


def _dot_split_rhs(m, x, n):
    parts = _split3(x)[:n]
    out = _dot(m, parts[0])
    for p in parts[1:]:
        out = out + _dot(m, p)
    return out


def _rms(h, w):
    ms = jnp.mean(h * h, axis=-1, keepdims=True)
    return h * lax.rsqrt(ms + RMS_EPS) * w


def _log_sigmoid(x):
    return jnp.minimum(x, 0.0) - jnp.log1p(jnp.exp(-jnp.abs(x)))


def _softplus(x):
    return jnp.maximum(x, 0.0) + jnp.log1p(jnp.exp(-jnp.abs(x)))


def _silu(x):
    return x * jax.nn.sigmoid(x)


def _inproj_kernel(x_ref, meta_ref, nw_ref, wq_ref, wk_ref, wv_ref, ws_ref, wg_ref, wz_ref,
                   wgate_ref, sbias_ref, placeq_ref, placek_ref, onesq_ref, onesk_ref, tril_ref,
                   qT_ref, kaug_ref, vT_ref, gqkv_ref, z_ref, small_ref, gates_ref, carry_ref):
    i = pl.program_id(1)
    h = jnp.where(i < NF, meta_ref[...], x_ref[0])
    hn = _rms(h, nw_ref[...]).astype(BF16)

    small = _dot(hn, ws_ref[...])
    small_ref[0] = small

    @pl.when(i == 0)
    def _():
        carry_ref[...] = jnp.zeros_like(carry_ref)

    lf = _log_sigmoid(small + sbias_ref[...]) * LOG2E
    c = _dot_split_rhs(tril_ref[...], lf, 3) + carry_ref[0:1, :]
    carry_ref[...] = jnp.broadcast_to(c[TM - 1:TM, :], carry_ref.shape)
    hi = c.astype(BF16).astype(F32)
    r = c - hi
    mid = r.astype(BF16).astype(F32)
    lo = r - mid
    lane = lax.broadcasted_iota(jnp.int32, c.shape, 1)
    parts = jnp.where(lane < HEADS, hi, jnp.where(lane < 2 * HEADS, mid, lo)).astype(BF16)

    kaug = _dot(hn, wk_ref[...]) + _dot(parts, placek_ref[...]) + onesk_ref[...]
    row = lax.broadcasted_iota(jnp.int32, kaug.shape, 0)
    col = lax.broadcasted_iota(jnp.int32, kaug.shape, 1)
    inert = jnp.logical_and(i * TM + row < PAD_ROWS, (col % LANES) == A_PAD)
    kaug = jnp.where(inert, NEG, kaug).astype(BF16)
    for hd in range(HEADS):
        kaug_ref[0, hd] = kaug[:, hd * LANES:(hd + 1) * LANES]

    qaug = (_dot(hn, wq_ref[...]) * (HEAD_DIM ** -0.5 * LOG2E)
            + _dot(parts, placeq_ref[...]) + onesq_ref[...])
    qT_ref[0, :, 0] = qaug.T.reshape(HEADS, LANES, TM).astype(BF16)

    vT = _dot(hn, wv_ref[...]).T
    vT_ref[0, :, 0, 0:HEAD_DIM, :] = vT.reshape(HEADS, HEAD_DIM, TM).astype(BF16)
    ones_row = lax.broadcasted_iota(jnp.int32, (HEADS, VROWS - HEAD_DIM, TM), 1) == 0
    vT_ref[0, :, 0, HEAD_DIM:VROWS, :] = ones_row.astype(F32).astype(BF16)

    gqkv_ref[0] = _dot(hn, wg_ref[...])
    z_ref[0] = _dot(hn, wz_ref[...])
    gates_ref[0] = _dot(hn, wgate_ref[...])


def _fox_kernel(qT_ref, k_ref, vT_ref, o_ref, s_buf, p_buf):
    i = pl.program_id(2)
    qTs = [qT_ref[0, hh, 0] for hh in range(2)]
    nrc = TK // RC
    chunk = lambda r: slice(r * RC, (r + 1) * RC)

    def score_chunk(j, hh, r):
        off = pl.multiple_of(j * TK + r * RC, RC)
        return _dot(k_ref[0, hh, pl.ds(off, RC), :], qTs[hh])

    cmax0 = []
    for hh in range(2):
        cm = jnp.full((1, TQ), NEG, F32)
        for r in range(nrc):
            sc = score_chunk(0, hh, r)
            cm = jnp.maximum(cm, jnp.max(sc, axis=0, keepdims=True))
            s_buf[hh, chunk(r), :] = sc
        cmax0.append(cm)
    p_buf[...] = jnp.zeros_like(p_buf)

    def body(j, st):
        jp = jnp.maximum(j - 1, 0)
        m_new = [jnp.maximum(st[hh][0], st[hh][3]) for hh in range(2)]
        alpha = [jnp.exp2(st[hh][0] - m_new[hh]) for hh in range(2)]
        accs = [st[hh][2] * st[hh][1] for hh in range(2)]
        cm_next = [jnp.full((1, TQ), NEG, F32) for _ in range(2)]
        for r in range(nrc):
            for hh in range(2):
                accs[hh] = accs[hh] + _dot(vT_ref[0, hh, jp, :, chunk(r)], p_buf[hh, chunk(r), :])
                p_buf[hh, chunk(r), :] = jnp.exp2(s_buf[hh, chunk(r), :] - m_new[hh]).astype(BF16)
                sc = score_chunk(j + 1, hh, r)
                cm_next[hh] = jnp.maximum(cm_next[hh], jnp.max(sc, axis=0, keepdims=True))
                s_buf[hh, chunk(r), :] = sc
        return tuple((m_new[hh], accs[hh], alpha[hh], cm_next[hh]) for hh in range(2))

    init = tuple((jnp.full((1, TQ), NEG, F32), jnp.zeros((VROWS, TQ), F32),
                  jnp.ones((1, TQ), F32), cmax0[hh]) for hh in range(2))
    st = lax.fori_loop(0, i, body, init)

    causal = (lax.broadcasted_iota(jnp.int32, (TK, TQ), 0)
              <= lax.broadcasted_iota(jnp.int32, (TK, TQ), 1))
    ip = jnp.maximum(i - 1, 0)
    outs = []
    for hh in range(2):
        m, acc, a_prev, _ = st[hh]
        acc = a_prev * acc + _dot(vT_ref[0, hh, ip], p_buf[hh])
        sT = jnp.where(causal, s_buf[hh], NEG)
        m_new = jnp.maximum(m, jnp.max(sT, axis=0, keepdims=True))
        acc = jnp.exp2(m - m_new) * acc + _dot(vT_ref[0, hh, i], jnp.exp2(sT - m_new).astype(BF16))
        outs.append(acc[:HEAD_DIM] / acc[HEAD_DIM:HEAD_DIM + 1])
    oT = jnp.concatenate(outs, axis=0)
    o_ref[0] = oT.T.astype(o_ref.dtype)


def _nt(a, b):
    return lax.dot_general(a, b, (((1,), (1,)), ((), ())), preferred_element_type=F32)


def _tn(a, b):
    return lax.dot_general(a, b, (((0,), (0,)), ((), ())), preferred_element_type=F32)


def _gdn_kernel(gq_ref, small_ref, z_ref, convw_ref, gbias_ref, alog_ref, normw_ref,
                tril_ref, last_ref, eb_ref, eg_ref, bd_ref, bdmask_ref, tile_ref, hsel_ref,
                o_ref, convbuf, state, obuf):
    i = pl.program_id(1)

    @pl.when(i == 0)
    def _():
        convbuf[0:8, :] = jnp.zeros((8, 3 * WIDTH), F32)
        state[...] = jnp.zeros_like(state)

    convbuf[8:8 + GROWS, :] = gq_ref[0]
    w = convw_ref[...]
    xc = w[GDN_CONV - 1:GDN_CONV, :] * convbuf[8:8 + GROWS, :]
    for t in range(1, GDN_CONV):
        xc = xc + w[GDN_CONV - 1 - t:GDN_CONV - t, :] * convbuf[8 - t:8 - t + GROWS, :]
    convbuf[0:8, :] = convbuf[GROWS:GROWS + 8, :]
    xs = _silu(xc)
    q = xs[:, :WIDTH]
    k = xs[:, WIDTH:2 * WIDTH]
    v = xs[:, 2 * WIDTH:]
    bd = bd_ref[...]
    qn = q * lax.rsqrt(_dot_split_lhs(q * q, bd, 2) + RMS_EPS) * (HEAD_DIM ** -0.5)
    kn = k * lax.rsqrt(_dot_split_lhs(k * k, bd, 2) + RMS_EPS)

    small = small_ref[0]
    beta_e = _dot_split_lhs(jax.nn.sigmoid(small), eb_ref[...], 2)
    g = -jnp.exp(alog_ref[...]) * _softplus(small + gbias_ref[...])
    gc = _dot_split_rhs(tril_ref[...], g, 3)
    gl = _dot_split_rhs(last_ref[...], gc, 3)
    gc_e = _dot_split_lhs(gc, eg_ref[...], 3)
    gl_e = _dot_split_lhs(gl, eg_ref[...], 3)
    gcT = gc.T
    eg = jnp.exp(gc_e)
    qd = qn * eg
    kb = kn * beta_e
    kbe = kb * eg
    kt = kn * jnp.exp(gl_e - gc_e)
    vb = v * beta_e
    cd_e = jnp.exp(gl_e)

    nch = GROWS // CHUNK
    nquad = HEADS // QH
    rowi = lax.broadcasted_iota(jnp.int32, (CHUNK, QW), 0)
    posi = lax.broadcasted_iota(jnp.int32, (CHUNK, QW), 1) % HEAD_DIM
    tril = rowi >= posi
    strict = rowi > posi
    eye = (rowi == posi).astype(F32)
    bdm = bdmask_ref[...]
    bdm16 = bdm.astype(BF16)

    def bd4(x):
        return jnp.concatenate([x.astype(BF16)] * QH, axis=0) * bdm16

    def bdot(lhs, x):
        return _dot(lhs.astype(BF16), bd4(x))

    gsel = _dot_split_lhs(gcT[L_A:L_A + HEADS, :], tile_ref[...], 3) * hsel_ref[...]
    grow_all = jnp.sum(gsel, axis=0, keepdims=True)

    items = [(c, u) for c in range(nch) for u in range(nquad)]
    rows_of = lambda c: slice(c * CHUNK, (c + 1) * CHUNK)
    cols_of = lambda u: slice(u * QW, (u + 1) * QW)
    a_l, attn_l = {}, {}
    for (c, u) in items:
        rows, cols = rows_of(c), cols_of(u)
        grow = grow_all[:, c * WIDTH + u * QW:c * WIDTH + (u + 1) * QW]
        decay = jnp.where(tril, jnp.exp(jnp.where(tril, gc_e[rows, cols] - grow, 0.0)), 0.0)
        aa = _nt(jnp.concatenate([kb[rows, cols], qn[rows, cols]], axis=0).astype(BF16),
                 bd4(kn[rows, cols]))
        a_l[c, u] = jnp.where(strict, aa[:CHUNK] * decay, 0.0)
        attn_l[c, u] = aa[CHUNK:] * decay

    t_l = {it: eye - a_l[it] for it in items}
    p_l = {it: bdot(a_l[it], a_l[it]) for it in items}
    steps = int(math.log2(CHUNK)) - 1
    for s in range(steps):
        for it in items:
            if s + 1 < steps:
                r = bdot(jnp.concatenate([t_l[it], p_l[it]], axis=0), p_l[it])
                t_l[it] = t_l[it] + r[:CHUNK]
                p_l[it] = r[CHUNK:]
            else:
                t_l[it] = t_l[it] + bdot(t_l[it], p_l[it])

    val_l = {(c, u): bdot(t_l[c, u], vb[rows_of(c), cols_of(u)]) for (c, u) in items}
    kcd_l = {(c, u): bdot(t_l[c, u], kbe[rows_of(c), cols_of(u)]) for (c, u) in items}

    s_cur = [state[u] for u in range(nquad)]
    for c in range(nch):
        rows = rows_of(c)
        kq = [_dot(jnp.concatenate([kcd_l[c, u], qd[rows, cols_of(u)]], axis=0), s_cur[u])
              for u in range(nquad)]
        v_new = [val_l[c, u] - kq[u][:CHUNK] for u in range(nquad)]
        o_l = [kq[u][CHUNK:] + bdot(attn_l[c, u], v_new[u]) for u in range(nquad)]
        upd = [_tn(kt[rows, cols_of(u)], v_new[u]) for u in range(nquad)]
        for u in range(nquad):
            cd = cd_e[(c + 1) * CHUNK - 1:(c + 1) * CHUNK, cols_of(u)]
            s_cur[u] = (s_cur[u] * cd + upd[u]) * bdm
            obuf[rows, cols_of(u)] = o_l[u]
    for u in range(nquad):
        state[u] = s_cur[u]

    o = obuf[...]
    ms = _dot_split_lhs(o * o, bd, 2) * (1.0 / HEAD_DIM)
    o_ref[0] = (o * lax.rsqrt(ms + RMS_EPS) * normw_ref[...] * _silu(z_ref[0])).astype(o_ref.dtype)


def _merge_kernel(x_ref, meta_ref, ofox_ref, ogdn_ref, gates_ref, gbias_ref, wbf_ref, wbg_ref,
                  wout_ref, h2_ref):
    i = pl.program_id(1)
    h = jnp.where(i < NF, meta_ref[...], x_ref[0])
    y_fox = _dot(ofox_ref[0].astype(BF16), wbf_ref[...])
    y_gdn = _dot(ogdn_ref[0].astype(BF16), wbg_ref[...])
    gate = jax.nn.sigmoid(gates_ref[0] + gbias_ref[...])
    mix = gate[:, :D_MODEL] * y_fox + gate[:, D_MODEL:] * y_gdn
    h2 = h + _dot(mix.astype(BF16), wout_ref[...])
    row = lax.broadcasted_iota(jnp.int32, h2.shape, 0)
    h2_ref[0] = jnp.where(i * TM + row < PAD_ROWS, 0.0, h2)


def _ffn_kernel(h2_ref, nw_ref, wup_ref, convw_ref, convb_ref, wdown_ref, fw_ref, o_ref,
                ubuf, tail, act):
    i = pl.program_id(1)

    @pl.when(i == 0)
    def _():
        tail[...] = jnp.zeros_like(tail)

    h2 = h2_ref[0]
    hn = _rms(h2, nw_ref[...]).astype(BF16)

    def conv_cols(c0, width):
        ubuf[0:8, 0:width] = tail[:, c0:c0 + width]
        ubuf[8:8 + TM, 0:width] = _dot(hn, wup_ref[:, c0:c0 + width])
        tail[:, c0:c0 + width] = ubuf[TM:TM + 8, 0:width]
        w = convw_ref[:, c0:c0 + width]
        y = convb_ref[:, c0:c0 + width] + w[FFN_CONV - 1:FFN_CONV, :] * ubuf[8:8 + TM, 0:width]
        for t in range(1, FFN_CONV):
            y = y + w[FFN_CONV - 1 - t:FFN_CONV - t, :] * ubuf[8 - t:8 - t + TM, 0:width]
        return y

    for n in range(D_FF // FF_CHUNK):
        c0 = n * FF_CHUNK
        gate = conv_cols(c0, FF_CHUNK)
        up = conv_cols(D_FF + c0, FF_CHUNK)
        act[:, c0:c0 + FF_CHUNK] = (_silu(gate) * up).astype(BF16)

    h3 = h2 + _dot(act[...], wdown_ref[...])
    o_ref[0] = _rms(h3, fw_ref[...])


def _constants():
    placeq = np.zeros((LANES, HEADS * LANES), np.float32)
    placek = np.zeros((LANES, HEADS * LANES), np.float32)
    onesq = np.zeros((1, HEADS * LANES), np.float32)
    onesk = np.zeros((1, HEADS * LANES), np.float32)
    for hd in range(HEADS):
        for part in range(3):
            placeq[part * HEADS + hd, hd * LANES + A_QC + part] = 1.0
            placek[part * HEADS + hd, hd * LANES + A_KC + part] = -1.0
        onesq[0, hd * LANES + A_KC:hd * LANES + A_PAD + 1] = 1.0
        onesk[0, hd * LANES + A_QC:hd * LANES + A_QC + 3] = 1.0
    tril_tm = np.tril(np.ones((TM, TM), np.float32))
    r = np.arange(GROWS)
    same = (r[:, None] // CHUNK) == (r[None, :] // CHUNK)
    tril_g = (same & (r[:, None] >= r[None, :])).astype(np.float32)
    last_g = (r[None, :] == (r[:, None] // CHUNK) * CHUNK + CHUNK - 1).astype(np.float32)
    eb = np.zeros((LANES, WIDTH), np.float32)
    eg = np.zeros((LANES, WIDTH), np.float32)
    for hd in range(HEADS):
        eb[L_B + hd, hd * HEAD_DIM:(hd + 1) * HEAD_DIM] = 1.0
        eg[L_A + hd, hd * HEAD_DIM:(hd + 1) * HEAD_DIM] = 1.0
    cw = np.arange(WIDTH)
    bd = ((cw[:, None] // HEAD_DIM) == (cw[None, :] // HEAD_DIM)).astype(np.float32)
    c2 = np.arange(QW)
    bdmask = ((c2[:, None] // HEAD_DIM) == (c2[None, :] // HEAD_DIM)).astype(np.float32)
    col = np.arange((GROWS // CHUNK) * WIDTH)
    tile_g = ((r[:, None] // CHUNK == col[None, :] // WIDTH)
              & (r[:, None] % CHUNK == col[None, :] % HEAD_DIM)).astype(np.float32)
    hsel = (np.arange(HEADS)[:, None] == (col[None, :] % WIDTH) // HEAD_DIM).astype(np.float32)
    return dict(placeq=placeq, placek=placek, onesq=onesq, onesk=onesk, tril_tm=tril_tm,
                tril_g=tril_g, last_g=last_g, eb=eb, eg=eg, bd=bd, bdmask=bdmask,
                tile_g=tile_g, hsel=hsel)


def _spread_heads(w):
    d = w.shape[0]
    w = w.reshape(d, HEADS, HEAD_DIM)
    return jnp.pad(w, ((0, 0), (0, 0), (0, LANES - HEAD_DIM))).reshape(d, HEADS * LANES)


def _lane_row(vals, start, copies=1):
    out = jnp.zeros((1, LANES), F32)
    for c in range(copies):
        out = lax.dynamic_update_slice(out, vals.reshape(1, -1).astype(F32), (0, start + c * vals.shape[-1]))
    return out


def kernel(x, meta_tokens, w_in, fgt_bias, gdn_conv_w, gdn_a_log, gdn_dt_bias, gdn_norm_w, gate_bias,
           w_branch_fox, w_branch_gdn, w_out, norm_mix_w, norm_ffn_w, ffn_w_up, ffn_conv_w,
           ffn_conv_b, ffn_w_down, norm_final_w):
    bn, seq, d = x.shape
    assert d == D_MODEL and seq % TM == 0 and w_in.shape[0] == 1
    p_rows = seq + FRONT
    nt = p_rows // TM
    cst = _constants()

    w = w_in[0]
    o = 0
    wq, wk, wv = (w[:, o + j * WIDTH:o + (j + 1) * WIDTH] for j in range(3))
    o += 3 * WIDTH
    wf = w[:, o:o + HEADS]
    o += HEADS
    wg = w[:, o:o + 3 * WIDTH]
    o += 3 * WIDTH
    wz = w[:, o:o + WIDTH]
    o += WIDTH
    wb = w[:, o:o + HEADS]
    o += HEADS
    wa = w[:, o:o + HEADS]
    o += HEADS
    wgate = w[:, o:]
    ws = jnp.concatenate([wf, wf, wf, wb, wa, jnp.zeros((d, LANES - 5 * HEADS), F32)], axis=1)
    meta_tile = jnp.concatenate([jnp.zeros((PAD_ROWS, d), F32), meta_tokens.astype(F32)], axis=0)
    sbias = _lane_row(fgt_bias[0], L_F, copies=3)
    gbias = _lane_row(gdn_dt_bias[0], L_A)
    alog = _lane_row(gdn_a_log[0], L_A)
    row = lambda a: a.reshape(1, -1).astype(F32)
    b16 = lambda a: jnp.asarray(a).astype(BF16)

    cparams = lambda sem: pltpu.CompilerParams(dimension_semantics=sem, vmem_limit_bytes=VMEM_LIMIT)
    x_spec = pl.BlockSpec((1, TM, d), lambda b, i: (b, jnp.maximum(i - NF, 0), 0))
    meta_spec = pl.BlockSpec((TM, d), lambda b, i: (jnp.minimum(i, NF - 1), 0))
    tile = lambda width: pl.BlockSpec((1, TM, width), lambda b, i: (b, i, 0))

    k1_in = [x, meta_tile, row(norm_mix_w[0]), b16(_spread_heads(wq)), b16(_spread_heads(wk)), b16(wv),
             b16(ws), b16(wg), b16(wz), b16(wgate), sbias, b16(cst["placeq"]), b16(cst["placek"]),
             jnp.asarray(cst["onesq"]), jnp.asarray(cst["onesk"]), b16(cst["tril_tm"])]
    k1_specs = [x_spec, meta_spec] + [_const_spec(a.shape) for a in k1_in[2:]]
    qT, kaug, vT, gqkv, z, small, gates = pl.pallas_call(
        _inproj_kernel,
        grid=(bn, nt),
        in_specs=k1_specs,
        out_specs=[
            pl.BlockSpec((1, HEADS, 1, LANES, TM), lambda b, i: (b, 0, i // TPA, 0, i % TPA)),
            pl.BlockSpec((1, HEADS, TM, LANES), lambda b, i: (b, 0, i, 0)),
            pl.BlockSpec((1, HEADS, 1, VROWS, TM), lambda b, i: (b, 0, i // TPA, 0, i % TPA)),
            tile(3 * WIDTH), tile(WIDTH), tile(LANES), tile(2 * D_MODEL)],
        out_shape=[
            jax.ShapeDtypeStruct((bn, HEADS, p_rows // TQ, LANES, TQ), BF16),
            jax.ShapeDtypeStruct((bn, HEADS, p_rows, LANES), BF16),
            jax.ShapeDtypeStruct((bn, HEADS, p_rows // TK, VROWS, TK), BF16),
            jax.ShapeDtypeStruct((bn, p_rows, 3 * WIDTH), F32),
            jax.ShapeDtypeStruct((bn, p_rows, WIDTH), F32),
            jax.ShapeDtypeStruct((bn, p_rows, LANES), F32),
            jax.ShapeDtypeStruct((bn, p_rows, 2 * D_MODEL), F32)],
        scratch_shapes=[pltpu.VMEM((8, LANES), F32)],
        compiler_params=cparams(("arbitrary", "arbitrary")),
        name="inproj",
    )(*k1_in)

    nq = p_rows // TQ
    o_fox = pl.pallas_call(
        _fox_kernel,
        grid=(bn, HEADS // 2, nq),
        in_specs=[
            pl.BlockSpec((1, 2, 1, LANES, TQ), lambda b, p, i: (b, p, i, 0, 0)),
            pl.BlockSpec((1, 2, p_rows, LANES), lambda b, p, i: (b, p, 0, 0)),
            pl.BlockSpec((1, 2, p_rows // TK, VROWS, TK), lambda b, p, i: (b, p, 0, 0, 0))],
        out_specs=pl.BlockSpec((1, TQ, 2 * HEAD_DIM), lambda b, p, i: (b, i, p)),
        out_shape=jax.ShapeDtypeStruct((bn, p_rows, WIDTH), BF16),
        scratch_shapes=[pltpu.VMEM((2, TK, TQ), F32), pltpu.VMEM((2, TK, TQ), BF16)],
        compiler_params=cparams(("arbitrary", "arbitrary", "arbitrary")),
        name="fox_attention",
    )(qT, kaug, vT)

    gtile = lambda width: pl.BlockSpec((1, GROWS, width), lambda b, i: (b, i, 0))
    k3_const = [gdn_conv_w[0].astype(F32), gbias, alog, row(jnp.tile(gdn_norm_w[0], HEADS)),
                b16(cst["tril_g"]), b16(cst["last_g"]), b16(cst["eb"]), b16(cst["eg"]), b16(cst["bd"]),
                jnp.asarray(cst["bdmask"]), b16(cst["tile_g"]), jnp.asarray(cst["hsel"])]
    o_gdn = pl.pallas_call(
        _gdn_kernel,
        grid=(bn, p_rows // GROWS),
        in_specs=[gtile(3 * WIDTH), gtile(LANES), gtile(WIDTH)] + [_const_spec(a.shape) for a in k3_const],
        out_specs=gtile(WIDTH),
        out_shape=jax.ShapeDtypeStruct((bn, p_rows, WIDTH), BF16),
        scratch_shapes=[pltpu.VMEM((GROWS + 8, 3 * WIDTH), F32),
                        pltpu.VMEM((HEADS // QH, QW, QW), F32),
                        pltpu.VMEM((GROWS, WIDTH), F32)],
        compiler_params=cparams(("arbitrary", "arbitrary")),
        name="gated_deltanet",
    )(gqkv, small, z, *k3_const)

    k4_const = [row(gate_bias[0]), b16(w_branch_fox[0]), b16(w_branch_gdn[0]), b16(w_out[0])]
    h2 = pl.pallas_call(
        _merge_kernel,
        grid=(bn, nt),
        in_specs=[x_spec, meta_spec, tile(WIDTH), tile(WIDTH), tile(2 * D_MODEL)]
        + [_const_spec(a.shape) for a in k4_const],
        out_specs=tile(d),
        out_shape=jax.ShapeDtypeStruct((bn, p_rows, d), F32),
        compiler_params=cparams(("arbitrary", "arbitrary")),
        name="merge",
    )(x, meta_tile, o_fox, o_gdn, gates, *k4_const)

    k5_const = [row(norm_ffn_w[0]), b16(ffn_w_up[0]), ffn_conv_w[0].astype(F32), row(ffn_conv_b[0]),
                b16(ffn_w_down[0]), row(norm_final_w)]
    out = pl.pallas_call(
        _ffn_kernel,
        grid=(bn, nt),
        in_specs=[tile(d)] + [_const_spec(a.shape) for a in k5_const],
        out_specs=pl.BlockSpec((1, TM, d), lambda b, i: (b, jnp.maximum(i - NF, 0), 0)),
        out_shape=jax.ShapeDtypeStruct((bn, seq, d), F32),
        scratch_shapes=[pltpu.VMEM((TM + 8, FF_CHUNK), F32),
                        pltpu.VMEM((8, 2 * D_FF), F32),
                        pltpu.VMEM((TM, D_FF), BF16)],
        compiler_params=cparams(("arbitrary", "arbitrary")),
        name="conv_ffn",
    )(h2, *k5_const)
    return out
```

```python
import functools
import math

import jax
import jax.numpy as jnp
import numpy as np
from jax import lax
from jax.experimental import pallas as pl
from jax.experimental.pallas import tpu as pltpu

F32 = jnp.float32
BF16 = jnp.bfloat16

D_MODEL = 1024
N_META = 16
HEADS = 8
HEAD_DIM = 64
WIDTH = HEADS * HEAD_DIM
GDN_CONV = 4
D_FF = 2816
FFN_CONV = 3
CHUNK = 64
RMS_EPS = 1e-6
NEG = -1e30
LOG2E = math.log2(math.e)

LANES = 128
FRONT = 512
PAD_ROWS = FRONT - N_META
TM = 256
NF = FRONT // TM
TQ = 512
TK = 512
TPA = TQ // TM
RC = 256
NHA = 4
VROWS = HEAD_DIM + 16
GROWS = 4 * CHUNK
QH = 4
QW = QH * HEAD_DIM
FF_CHUNK = 256
VMEM_LIMIT = 56 * 1024 * 1024

L_F = 0
L_B = 24
L_A = 32
A_QC = 64
A_KC = 67
A_PAD = 70


def _const_spec(shape):
    n = len(shape)
    return pl.BlockSpec(shape, lambda *_: (0,) * n, pipeline_mode=pl.Buffered(1))


def _split3(x):
    hi = x.astype(BF16)
    r = x - hi.astype(F32)
    mid = r.astype(BF16)
    lo = (r - mid.astype(F32)).astype(BF16)
    return hi, mid, lo


def _dot(a, b):
    return jnp.dot(a, b, preferred_element_type=F32)


def _dot_split_lhs(x, m, n):
    parts = _split3(x)[:n]
    out = _dot(parts[0], m)
    for p in parts[1:]:
        out = out + _dot(p, m)
    return out


def _dot_split_rhs(m, x, n):
    parts = _split3(x)[:n]
    out = _dot(m, parts[0])
    for p in parts[1:]:
        out = out + _dot(m, p)
    return out


def _rms(h, w):
    ms = jnp.mean(h * h, axis=-1, keepdims=True)
    return h * lax.rsqrt(ms + RMS_EPS) * w


def _log_sigmoid(x):
    return jnp.minimum(x, 0.0) - jnp.log1p(jnp.exp(-jnp.abs(x)))


def _softplus(x):
    return jnp.maximum(x, 0.0) + jnp.log1p(jnp.exp(-jnp.abs(x)))


def _silu(x):
    return x * jax.nn.sigmoid(x)


def _inproj_kernel(x_ref, meta_ref, nw_ref, wq_ref, wk_ref, wv_ref, ws_ref, wg_ref, wz_ref,
                   wgate_ref, sbias_ref, placeq_ref, placek_ref, onesq_ref, onesk_ref, tril_ref,
                   qT_ref, kaug_ref, vT_ref, gqkv_ref, z_ref, small_ref, gates_ref, carry_ref):
    i = pl.program_id(1)
    h = jnp.where(i < NF, meta_ref[...], x_ref[0])
    hn = _rms(h, nw_ref[...]).astype(BF16)

    small = _dot(hn, ws_ref[...])
    small_ref[0] = small

    @pl.when(i == 0)
    def _():
        carry_ref[...] = jnp.zeros_like(carry_ref)

    lf = _log_sigmoid(small + sbias_ref[...]) * LOG2E
    c = _dot_split_rhs(tril_ref[...], lf, 3) + carry_ref[0:1, :]
    carry_ref[...] = jnp.broadcast_to(c[TM - 1:TM, :], carry_ref.shape)
    hi = c.astype(BF16).astype(F32)
    r = c - hi
    mid = r.astype(BF16).astype(F32)
    lo = r - mid
    lane = lax.broadcasted_iota(jnp.int32, c.shape, 1)
    parts = jnp.where(lane < HEADS, hi, jnp.where(lane < 2 * HEADS, mid, lo)).astype(BF16)

    kaug = _dot(hn, wk_ref[...]) + _dot(parts, placek_ref[...]) + onesk_ref[...]
    row = lax.broadcasted_iota(jnp.int32, kaug.shape, 0)
    col = lax.broadcasted_iota(jnp.int32, kaug.shape, 1)
    inert = jnp.logical_and(i * TM + row < PAD_ROWS, (col % LANES) == A_PAD)
    kaug = jnp.where(inert, NEG, kaug).astype(BF16)
    for hd in range(HEADS):
        kaug_ref[0, hd] = kaug[:, hd * LANES:(hd + 1) * LANES]

    qaug = (_dot(hn, wq_ref[...]) * (HEAD_DIM ** -0.5 * LOG2E)
            + _dot(parts, placeq_ref[...]) + onesq_ref[...])
    qT_ref[0, :, 0] = qaug.T.reshape(HEADS, LANES, TM).astype(BF16)

    vT = _dot(hn, wv_ref[...]).T
    vT_ref[0, :, 0, 0:HEAD_DIM, :] = vT.reshape(HEADS, HEAD_DIM, TM).astype(BF16)
    ones_row = lax.broadcasted_iota(jnp.int32, (HEADS, VROWS - HEAD_DIM, TM), 1) == 0
    vT_ref[0, :, 0, HEAD_DIM:VROWS, :] = ones_row.astype(F32).astype(BF16)

    gqkv_ref[0] = _dot(hn, wg_ref[...])
    z_ref[0] = _dot(hn, wz_ref[...])
    gates_ref[0] = _dot(hn, wgate_ref[...]).astype(gates_ref.dtype)


def _fox_kernel(qT_ref, k_ref, vT_ref, o_ref, s_buf, p_buf):
    i = pl.program_id(2)
    qTs = [qT_ref[0, hh, 0] for hh in range(NHA)]
    nrc = TK // RC
    chunk = lambda r: slice(r * RC, (r + 1) * RC)

    def score_chunk(j, hh, r):
        off = pl.multiple_of(j * TK + r * RC, RC)
        return _dot(k_ref[0, hh, pl.ds(off, RC), :], qTs[hh])

    cmax0 = []
    for hh in range(NHA):
        cm = jnp.full((1, TQ), NEG, F32)
        for r in range(nrc):
            sc = score_chunk(0, hh, r)
            cm = jnp.maximum(cm, jnp.max(sc, axis=0, keepdims=True))
            s_buf[hh, chunk(r), :] = sc
        cmax0.append(cm)
    p_buf[...] = jnp.zeros_like(p_buf)

    def body(j, st):
        jp = jnp.maximum(j - 1, 0)
        m_new = [jnp.maximum(st[hh][0], st[hh][3]) for hh in range(NHA)]
        alpha = [jnp.exp2(st[hh][0] - m_new[hh]) for hh in range(NHA)]
        accs = [st[hh][2] * st[hh][1] for hh in range(NHA)]
        cm_next = [jnp.full((1, TQ), NEG, F32) for _ in range(NHA)]
        for r in range(nrc):
            for hh in range(NHA):
                accs[hh] = accs[hh] + _dot(vT_ref[0, hh, jp, :, chunk(r)], p_buf[hh, chunk(r), :])
                p_buf[hh, chunk(r), :] = jnp.exp2(s_buf[hh, chunk(r), :] - m_new[hh]).astype(BF16)
                sc = score_chunk(j + 1, hh, r)
                cm_next[hh] = jnp.maximum(cm_next[hh], jnp.max(sc, axis=0, keepdims=True))
                s_buf[hh, chunk(r), :] = sc
        return tuple((m_new[hh], accs[hh], alpha[hh], cm_next[hh]) for hh in range(NHA))

    init = tuple((jnp.full((1, TQ), NEG, F32), jnp.zeros((VROWS, TQ), F32),
                  jnp.ones((1, TQ), F32), cmax0[hh]) for hh in range(NHA))
    st = lax.fori_loop(0, i, body, init)

    causal = (lax.broadcasted_iota(jnp.int32, (TK, TQ), 0)
              <= lax.broadcasted_iota(jnp.int32, (TK, TQ), 1))
    ip = jnp.maximum(i - 1, 0)
    outs = []
    for hh in range(NHA):
        m, acc, a_prev, _ = st[hh]
        acc = a_prev * acc + _dot(vT_ref[0, hh, ip], p_buf[hh])
        sT = jnp.where(causal, s_buf[hh], NEG)
        m_new = jnp.maximum(m, jnp.max(sT, axis=0, keepdims=True))
        acc = jnp.exp2(m - m_new) * acc + _dot(vT_ref[0, hh, i], jnp.exp2(sT - m_new).astype(BF16))
        outs.append(acc[:HEAD_DIM] / acc[HEAD_DIM:HEAD_DIM + 1])
    oT = jnp.concatenate(outs, axis=0)
    o_ref[0] = oT.T.astype(o_ref.dtype)


def _nt(a, b):
    return lax.dot_general(a, b, (((1,), (1,)), ((), ())), preferred_element_type=F32)


def _tn(a, b):
    return lax.dot_general(a, b, (((0,), (0,)), ((), ())), preferred_element_type=F32)


def _gdn_kernel(gq_ref, small_ref, z_ref, convw_ref, gbias_ref, alog_ref, normw_ref,
                tril_ref, last_ref, eb_ref, eg_ref, bd_ref, bdmask_ref, tile_ref, hsel_ref,
                o_ref, convbuf, state, obuf):
    i = pl.program_id(1)

    @pl.when(i == 0)
    def _():
        convbuf[0:8, :] = jnp.zeros((8, 3 * WIDTH), F32)
        state[...] = jnp.zeros_like(state)

    convbuf[8:8 + GROWS, :] = gq_ref[0]
    w = convw_ref[...]
    xc = w[GDN_CONV - 1:GDN_CONV, :] * convbuf[8:8 + GROWS, :]
    for t in range(1, GDN_CONV):
        xc = xc + w[GDN_CONV - 1 - t:GDN_CONV - t, :] * convbuf[8 - t:8 - t + GROWS, :]
    convbuf[0:8, :] = convbuf[GROWS:GROWS + 8, :]
    xs = _silu(xc)
    q = xs[:, :WIDTH]
    k = xs[:, WIDTH:2 * WIDTH]
    v = xs[:, 2 * WIDTH:]
    bd = bd_ref[...]
    qn = q * lax.rsqrt(_dot_split_lhs(q * q, bd, 2) + RMS_EPS) * (HEAD_DIM ** -0.5)
    kn = k * lax.rsqrt(_dot_split_lhs(k * k, bd, 2) + RMS_EPS)

    small = small_ref[0]
    beta_e = _dot_split_lhs(jax.nn.sigmoid(small), eb_ref[...], 2)
    g = -jnp.exp(alog_ref[...]) * _softplus(small + gbias_ref[...])
    gc = _dot_split_rhs(tril_ref[...], g, 3)
    gl = _dot_split_rhs(last_ref[...], gc, 3)
    gc_e = _dot_split_lhs(gc, eg_ref[...], 3)
    gl_e = _dot_split_lhs(gl, eg_ref[...], 3)
    gcT = gc.T
    eg = jnp.exp(gc_e)
    qd = qn * eg
    kb = kn * beta_e
    kbe = kb * eg
    kt = kn * jnp.exp(gl_e - gc_e)
    vb = v * beta_e
    cd_e = jnp.exp(gl_e)

    nch = GROWS // CHUNK
    nquad = HEADS // QH
    rowi = lax.broadcasted_iota(jnp.int32, (CHUNK, QW), 0)
    posi = lax.broadcasted_iota(jnp.int32, (CHUNK, QW), 1) % HEAD_DIM
    tril = rowi >= posi
    strict = rowi > posi
    eye = (rowi == posi).astype(F32)
    bdm = bdmask_ref[...]
    bdm16 = bdm.astype(BF16)

    def bd4(x):
        return jnp.concatenate([x.astype(BF16)] * QH, axis=0) * bdm16

    def bdot(lhs, x):
        return _dot(lhs.astype(BF16), bd4(x))

    gsel = _dot_split_lhs(gcT[L_A:L_A + HEADS, :], tile_ref[...], 3) * hsel_ref[...]
    grow_all = jnp.sum(gsel, axis=0, keepdims=True)

    items = [(c, u) for c in range(nch) for u in range(nquad)]
    rows_of = lambda c: slice(c * CHUNK, (c + 1) * CHUNK)
    cols_of = lambda u: slice(u * QW, (u + 1) * QW)
    a_l, attn_l = {}, {}
    for (c, u) in items:
        rows, cols = rows_of(c), cols_of(u)
        grow = grow_all[:, c * WIDTH + u * QW:c * WIDTH + (u + 1) * QW]
        decay = jnp.where(tril, jnp.exp(jnp.where(tril, gc_e[rows, cols] - grow, 0.0)), 0.0)
        aa = _nt(jnp.concatenate([kb[rows, cols], qn[rows, cols]], axis=0).astype(BF16),
                 bd4(kn[rows, cols]))
        a_l[c, u] = jnp.where(strict, aa[:CHUNK] * decay, 0.0)
        attn_l[c, u] = aa[CHUNK:] * decay

    t_l = {it: eye - a_l[it] for it in items}
    p_l = {it: bdot(a_l[it], a_l[it]) for it in items}
    steps = int(math.log2(CHUNK)) - 1
    for s in range(steps):
        for it in items:
            if s + 1 < steps:
                r = bdot(jnp.concatenate([t_l[it], p_l[it]], axis=0), p_l[it])
                t_l[it] = t_l[it] + r[:CHUNK]
                p_l[it] = r[CHUNK:]
            else:
                t_l[it] = t_l[it] + bdot(t_l[it], p_l[it])

    val_l = {(c, u): bdot(t_l[c, u], vb[rows_of(c), cols_of(u)]) for (c, u) in items}
    kcd_l = {(c, u): bdot(t_l[c, u], kbe[rows_of(c), cols_of(u)]) for (c, u) in items}

    s_cur = [state[u] for u in range(nquad)]
    for c in range(nch):
        rows = rows_of(c)
        kq = [_dot(jnp.concatenate([kcd_l[c, u], qd[rows, cols_of(u)]], axis=0), s_cur[u])
              for u in range(nquad)]
        v_new = [val_l[c, u] - kq[u][:CHUNK] for u in range(nquad)]
        o_l = [kq[u][CHUNK:] + bdot(attn_l[c, u], v_new[u]) for u in range(nquad)]
        upd = [_tn(kt[rows, cols_of(u)], v_new[u]) for u in range(nquad)]
        for u in range(nquad):
            cd = cd_e[(c + 1) * CHUNK - 1:(c + 1) * CHUNK, cols_of(u)]
            s_cur[u] = (s_cur[u] * cd + upd[u]) * bdm
            obuf[rows, cols_of(u)] = o_l[u]
    for u in range(nquad):
        state[u] = s_cur[u]

    o = obuf[...]
    ms = _dot_split_lhs(o * o, bd, 2) * (1.0 / HEAD_DIM)
    o_ref[0] = (o * lax.rsqrt(ms + RMS_EPS) * normw_ref[...] * _silu(z_ref[0])).astype(o_ref.dtype)


def _merge_ffn_kernel(x_ref, meta_ref, ofox_ref, ogdn_ref, gates_ref, gbias_ref, wbf_ref, wbg_ref,
                      wout_ref, nw_ref, wup_ref, convw_ref, convb_ref, wdown_ref, fw_ref, o_ref,
                      ubuf, tail, act):
    i = pl.program_id(1)

    @pl.when(i == 0)
    def _():
        tail[...] = jnp.zeros_like(tail)

    h = jnp.where(i < NF, meta_ref[...], x_ref[0])
    y_fox = _dot(ofox_ref[0], wbf_ref[...])
    y_gdn = _dot(ogdn_ref[0], wbg_ref[...])
    gate = jax.nn.sigmoid(gates_ref[0].astype(F32) + gbias_ref[...])
    mix = gate[:, :D_MODEL] * y_fox + gate[:, D_MODEL:] * y_gdn
    h2 = h + _dot(mix.astype(BF16), wout_ref[...])
    row = lax.broadcasted_iota(jnp.int32, h2.shape, 0)
    h2 = jnp.where(i * TM + row < PAD_ROWS, 0.0, h2)
    hn = _rms(h2, nw_ref[...]).astype(BF16)

    def conv_cols(c0, width):
        ubuf[0:8, 0:width] = tail[:, c0:c0 + width]
        ubuf[8:8 + TM, 0:width] = _dot(hn, wup_ref[:, c0:c0 + width])
        tail[:, c0:c0 + width] = ubuf[TM:TM + 8, 0:width]
        w = convw_ref[:, c0:c0 + width]
        y = convb_ref[:, c0:c0 + width] + w[FFN_CONV - 1:FFN_CONV, :] * ubuf[8:8 + TM, 0:width]
        for t in range(1, FFN_CONV):
            y = y + w[FFN_CONV - 1 - t:FFN_CONV - t, :] * ubuf[8 - t:8 - t + TM, 0:width]
        return y

    for n in range(D_FF // FF_CHUNK):
        c0 = n * FF_CHUNK
        gate_u = conv_cols(c0, FF_CHUNK)
        up = conv_cols(D_FF + c0, FF_CHUNK)
        act[:, c0:c0 + FF_CHUNK] = (_silu(gate_u) * up).astype(BF16)

    h3 = h2 + _dot(act[...], wdown_ref[...])
    o_ref[0] = _rms(h3, fw_ref[...])


def _constants():
    placeq = np.zeros((LANES, HEADS * LANES), np.float32)
    placek = np.zeros((LANES, HEADS * LANES), np.float32)
    onesq = np.zeros((1, HEADS * LANES), np.float32)
    onesk = np.zeros((1, HEADS * LANES), np.float32)
    for hd in range(HEADS):
        for part in range(3):
            placeq[part * HEADS + hd, hd * LANES + A_QC + part] = 1.0
            placek[part * HEADS + hd, hd * LANES + A_KC + part] = -1.0
        onesq[0, hd * LANES + A_KC:hd * LANES + A_PAD + 1] = 1.0
        onesk[0, hd * LANES + A_QC:hd * LANES + A_QC + 3] = 1.0
    tril_tm = np.tril(np.ones((TM, TM), np.float32))
    r = np.arange(GROWS)
    same = (r[:, None] // CHUNK) == (r[None, :] // CHUNK)
    tril_g = (same & (r[:, None] >= r[None, :])).astype(np.float32)
    last_g = (r[None, :] == (r[:, None] // CHUNK) * CHUNK + CHUNK - 1).astype(np.float32)
    eb = np.zeros((LANES, WIDTH), np.float32)
    eg = np.zeros((LANES, WIDTH), np.float32)
    for hd in range(HEADS):
        eb[L_B + hd, hd * HEAD_DIM:(hd + 1) * HEAD_DIM] = 1.0
        eg[L_A + hd, hd * HEAD_DIM:(hd + 1) * HEAD_DIM] = 1.0
    cw = np.arange(WIDTH)
    bd = ((cw[:, None] // HEAD_DIM) == (cw[None, :] // HEAD_DIM)).astype(np.float32)
    c2 = np.arange(QW)
    bdmask = ((c2[:, None] // HEAD_DIM) == (c2[None, :] // HEAD_DIM)).astype(np.float32)
    col = np.arange((GROWS // CHUNK) * WIDTH)
    tile_g = ((r[:, None] // CHUNK == col[None, :] // WIDTH)
              & (r[:, None] % CHUNK == col[None, :] % HEAD_DIM)).astype(np.float32)
    hsel = (np.arange(HEADS)[:, None] == (col[None, :] % WIDTH) // HEAD_DIM).astype(np.float32)
    return dict(placeq=placeq, placek=placek, onesq=onesq, onesk=onesk, tril_tm=tril_tm,
                tril_g=tril_g, last_g=last_g, eb=eb, eg=eg, bd=bd, bdmask=bdmask,
                tile_g=tile_g, hsel=hsel)


def _spread_heads(w):
    d = w.shape[0]
    w = w.reshape(d, HEADS, HEAD_DIM)
    return jnp.pad(w, ((0, 0), (0, 0), (0, LANES - HEAD_DIM))).reshape(d, HEADS * LANES)


def _lane_row(vals, start, copies=1):
    out = jnp.zeros((1, LANES), F32)
    for c in range(copies):
        out = lax.dynamic_update_slice(out, vals.reshape(1, -1).astype(F32), (0, start + c * vals.shape[-1]))
    return out


def kernel(x, meta_tokens, w_in, fgt_bias, gdn_conv_w, gdn_a_log, gdn_dt_bias, gdn_norm_w, gate_bias,
           w_branch_fox, w_branch_gdn, w_out, norm_mix_w, norm_ffn_w, ffn_w_up, ffn_conv_w,
           ffn_conv_b, ffn_w_down, norm_final_w):
    bn, seq, d = x.shape
    assert d == D_MODEL and seq % TM == 0 and w_in.shape[0] == 1
    p_rows = seq + FRONT
    nt = p_rows // TM
    cst = _constants()

    w = w_in[0]
    o = 0
    wq, wk, wv = (w[:, o + j * WIDTH:o + (j + 1) * WIDTH] for j in range(3))
    o += 3 * WIDTH
    wf = w[:, o:o + HEADS]
    o += HEADS
    wg = w[:, o:o + 3 * WIDTH]
    o += 3 * WIDTH
    wz = w[:, o:o + WIDTH]
    o += WIDTH
    wb = w[:, o:o + HEADS]
    o += HEADS
    wa = w[:, o:o + HEADS]
    o += HEADS
    wgate = w[:, o:]
    ws = jnp.concatenate([wf, wf, wf, wb, wa, jnp.zeros((d, LANES - 5 * HEADS), F32)], axis=1)
    meta_tile = jnp.concatenate([jnp.zeros((PAD_ROWS, d), F32), meta_tokens.astype(F32)], axis=0)
    sbias = _lane_row(fgt_bias[0], L_F, copies=3)
    gbias = _lane_row(gdn_dt_bias[0], L_A)
    alog = _lane_row(gdn_a_log[0], L_A)
    row = lambda a: a.reshape(1, -1).astype(F32)
    b16 = lambda a: jnp.asarray(a).astype(BF16)

    cparams = lambda sem: pltpu.CompilerParams(dimension_semantics=sem, vmem_limit_bytes=VMEM_LIMIT)
    x_spec = pl.BlockSpec((1, TM, d), lambda b, i: (b, jnp.maximum(i - NF, 0), 0))
    meta_spec = pl.BlockSpec((TM, d), lambda b, i: (jnp.minimum(i, NF - 1), 0))
    tile = lambda width: pl.BlockSpec((1, TM, width), lambda b, i: (b, i, 0))

    k1_in = [x, meta_tile, row(norm_mix_w[0]), b16(_spread_heads(wq)), b16(_spread_heads(wk)), b16(wv),
             b16(ws), b16(wg), b16(wz), b16(wgate), sbias, b16(cst["placeq"]), b16(cst["placek"]),
             jnp.asarray(cst["onesq"]), jnp.asarray(cst["onesk"]), b16(cst["tril_tm"])]
    k1_specs = [x_spec, meta_spec] + [_const_spec(a.shape) for a in k1_in[2:]]
    qT, kaug, vT, gqkv, z, small, gates = pl.pallas_call(
        _inproj_kernel,
        grid=(bn, nt),
        in_specs=k1_specs,
        out_specs=[
            pl.BlockSpec((1, HEADS, 1, LANES, TM), lambda b, i: (b, 0, i // TPA, 0, i % TPA)),
            pl.BlockSpec((1, HEADS, TM, LANES), lambda b, i: (b, 0, i, 0)),
            pl.BlockSpec((1, HEADS, 1, VROWS, TM), lambda b, i: (b, 0, i // TPA, 0, i % TPA)),
            tile(3 * WIDTH), tile(WIDTH), tile(LANES), tile(2 * D_MODEL)],
        out_shape=[
            jax.ShapeDtypeStruct((bn, HEADS, p_rows // TQ, LANES, TQ), BF16),
            jax.ShapeDtypeStruct((bn, HEADS, p_rows, LANES), BF16),
            jax.ShapeDtypeStruct((bn, HEADS, p_rows // TK, VROWS, TK), BF16),
            jax.ShapeDtypeStruct((bn, p_rows, 3 * WIDTH), F32),
            jax.ShapeDtypeStruct((bn, p_rows, WIDTH), F32),
            jax.ShapeDtypeStruct((bn, p_rows, LANES), F32),
            jax.ShapeDtypeStruct((bn, p_rows, 2 * D_MODEL), BF16)],
        scratch_shapes=[pltpu.VMEM((8, LANES), F32)],
        compiler_params=cparams(("arbitrary", "arbitrary")),
        name="inproj",
    )(*k1_in)

    nq = p_rows // TQ
    o_fox = pl.pallas_call(
        _fox_kernel,
        grid=(bn, HEADS // NHA, nq),
        in_specs=[
            pl.BlockSpec((1, NHA, 1, LANES, TQ), lambda b, p, i: (b, p, i, 0, 0)),
            pl.BlockSpec((1, NHA, p_rows, LANES), lambda b, p, i: (b, p, 0, 0),
                         pipeline_mode=pl.Buffered(1)),
            pl.BlockSpec((1, NHA, p_rows // TK, VROWS, TK), lambda b, p, i: (b, p, 0, 0, 0),
                         pipeline_mode=pl.Buffered(1))],
        out_specs=pl.BlockSpec((1, TQ, NHA * HEAD_DIM), lambda b, p, i: (b, i, p)),
        out_shape=jax.ShapeDtypeStruct((bn, p_rows, WIDTH), BF16),
        scratch_shapes=[pltpu.VMEM((NHA, TK, TQ), F32), pltpu.VMEM((NHA, TK, TQ), BF16)],
        compiler_params=cparams(("arbitrary", "arbitrary", "arbitrary")),
        name="fox_attention",
    )(qT, kaug, vT)

    gtile = lambda width: pl.BlockSpec((1, GROWS, width), lambda b, i: (b, i, 0))
    k3_const = [gdn_conv_w[0].astype(F32), gbias, alog, row(jnp.tile(gdn_norm_w[0], HEADS)),
                b16(cst["tril_g"]), b16(cst["last_g"]), b16(cst["eb"]), b16(cst["eg"]), b16(cst["bd"]),
                jnp.asarray(cst["bdmask"]), b16(cst["tile_g"]), jnp.asarray(cst["hsel"])]
    o_gdn = pl.pallas_call(
        _gdn_kernel,
        grid=(bn, p_rows // GROWS),
        in_specs=[gtile(3 * WIDTH), gtile(LANES), gtile(WIDTH)] + [_const_spec(a.shape) for a in k3_const],
        out_specs=gtile(WIDTH),
        out_shape=jax.ShapeDtypeStruct((bn, p_rows, WIDTH), BF16),
        scratch_shapes=[pltpu.VMEM((GROWS + 8, 3 * WIDTH), F32),
                        pltpu.VMEM((HEADS // QH, QW, QW), F32),
                        pltpu.VMEM((GROWS, WIDTH), F32)],
        compiler_params=cparams(("arbitrary", "arbitrary")),
        name="gated_deltanet",
    )(gqkv, small, z, *k3_const)

    k4_const = [row(gate_bias[0]), b16(w_branch_fox[0]), b16(w_branch_gdn[0]), b16(w_out[0]),
                row(norm_ffn_w[0]), b16(ffn_w_up[0]), ffn_conv_w[0].astype(F32), row(ffn_conv_b[0]),
                b16(ffn_w_down[0]), row(norm_final_w)]
    out = pl.pallas_call(
        _merge_ffn_kernel,
        grid=(bn, nt),
        in_specs=[x_spec, meta_spec, tile(WIDTH), tile(WIDTH), tile(2 * D_MODEL)]
        + [_const_spec(a.shape) for a in k4_const],
        out_specs=pl.BlockSpec((1, TM, d), lambda b, i: (b, jnp.maximum(i - NF, 0), 0)),
        out_shape=jax.ShapeDtypeStruct((bn, seq, d), F32),
        scratch_shapes=[pltpu.VMEM((TM + 8, FF_CHUNK), F32),
                        pltpu.VMEM((8, 2 * D_FF), F32),
                        pltpu.VMEM((TM, D_FF), BF16)],
        compiler_params=cparams(("arbitrary", "arbitrary")),
        name="merge_ffn",
    )(x, meta_tile, o_fox, o_gdn, gates, *k4_const)
    return out
```

```python
import functools
import math

import jax
import jax.numpy as jnp
import numpy as np
from jax import lax
from jax.experimental import pallas as pl
from jax.experimental.pallas import tpu as pltpu

F32 = jnp.float32
BF16 = jnp.bfloat16

D_MODEL = 1024
N_META = 16
HEADS = 8
HEAD_DIM = 64
WIDTH = HEADS * HEAD_DIM
GDN_CONV = 4
D_FF = 2816
FFN_CONV = 3
CHUNK = 64
RMS_EPS = 1e-6
NEG = -1e30
LOG2E = math.log2(math.e)

LANES = 128
FRONT = 512
PAD_ROWS = FRONT - N_META
TM = 256
NF = FRONT // TM
TQ = 512
TK = 512
TPA = TQ // TM
RC = 256
NHA = 4
VROWS = HEAD_DIM + 16
GROWS = 4 * CHUNK
QH = 4
QW = QH * HEAD_DIM
FF_CHUNK = 256
VMEM_LIMIT = 56 * 1024 * 1024

L_F = 0
L_B = 24
L_A = 32
A_QC = 64
A_KC = 67
A_PAD = 70


def _const_spec(shape):
    n = len(shape)
    return pl.BlockSpec(shape, lambda *_: (0,) * n, pipeline_mode=pl.Buffered(1))


def _split3(x):
    hi = x.astype(BF16)
    r = x - hi.astype(F32)
    mid = r.astype(BF16)
    lo = (r - mid.astype(F32)).astype(BF16)
    return hi, mid, lo


def _dot(a, b):
    return jnp.dot(a, b, preferred_element_type=F32)


def _dot_split_lhs(x, m, n):
    parts = _split3(x)[:n]
    out = _dot(parts[0], m)
    for p in parts[1:]:
        out = out + _dot(p, m)
    return out


def _dot_split_rhs(m, x, n):
    parts = _split3(x)[:n]
    out = _dot(m, parts[0])
    for p in parts[1:]:
        out = out + _dot(m, p)
    return out


def _rms(h, w):
    ms = jnp.mean(h * h, axis=-1, keepdims=True)
    return h * lax.rsqrt(ms + RMS_EPS) * w


def _log_sigmoid(x):
    return jnp.minimum(x, 0.0) - jnp.log1p(jnp.exp(-jnp.abs(x)))


def _softplus(x):
    return jnp.maximum(x, 0.0) + jnp.log1p(jnp.exp(-jnp.abs(x)))


def _silu(x):
    return x * jax.nn.sigmoid(x)


def _inproj_kernel(x_ref, meta_ref, nw_ref, wq_ref, wk_ref, wv_ref, ws_ref, wg_ref, wz_ref,
                   wgate_ref, sbias_ref, placeq_ref, placek_ref, onesq_ref, onesk_ref, tril_ref,
                   qT_ref, kaug_ref, vT_ref, gqkv_ref, z_ref, small_ref, gates_ref, carry_ref):
    i = pl.program_id(1)
    h = jnp.where(i < NF, meta_ref[...], x_ref[0])
    hn = _rms(h, nw_ref[...]).astype(BF16)

    small = _dot(hn, ws_ref[...])
    small_ref[0] = small

    @pl.when(i == 0)
    def _():
        carry_ref[...] = jnp.zeros_like(carry_ref)

    lf = _log_sigmoid(small + sbias_ref[...]) * LOG2E
    c = _dot_split_rhs(tril_ref[...], lf, 3) + carry_ref[0:1, :]
    carry_ref[...] = jnp.broadcast_to(c[TM - 1:TM, :], carry_ref.shape)
    hi = c.astype(BF16).astype(F32)
    r = c - hi
    mid = r.astype(BF16).astype(F32)
    lo = r - mid
    lane = lax.broadcasted_iota(jnp.int32, c.shape, 1)
    parts = jnp.where(lane < HEADS, hi, jnp.where(lane < 2 * HEADS, mid, lo)).astype(BF16)

    kaug = _dot(hn, wk_ref[...]) + _dot(parts, placek_ref[...]) + onesk_ref[...]
    row = lax.broadcasted_iota(jnp.int32, kaug.shape, 0)
    col = lax.broadcasted_iota(jnp.int32, kaug.shape, 1)
    inert = jnp.logical_and(i * TM + row < PAD_ROWS, (col % LANES) == A_PAD)
    kaug = jnp.where(inert, NEG, kaug).astype(BF16)
    for hd in range(HEADS):
        kaug_ref[0, hd] = kaug[:, hd * LANES:(hd + 1) * LANES]

    qT = (_dot(hn, wq_ref[...]) * (HEAD_DIM ** -0.5 * LOG2E)).T
    qT_ref[0, :, 0, 0:HEAD_DIM, :] = qT.reshape(HEADS, HEAD_DIM, TM).astype(BF16)
    augT = (_dot(parts, placeq_ref[...]) + onesq_ref[...]).T
    qT_ref[0, :, 0, HEAD_DIM:LANES, :] = augT.reshape(HEADS, LANES - HEAD_DIM, TM).astype(BF16)

    vT = _dot(hn, wv_ref[...]).T
    vT_ref[0, :, 0, 0:HEAD_DIM, :] = vT.reshape(HEADS, HEAD_DIM, TM).astype(BF16)
    ones_row = lax.broadcasted_iota(jnp.int32, (HEADS, VROWS - HEAD_DIM, TM), 1) == 0
    vT_ref[0, :, 0, HEAD_DIM:VROWS, :] = ones_row.astype(F32).astype(BF16)

    gqkv_ref[0] = _dot(hn, wg_ref[...])
    z_ref[0] = _dot(hn, wz_ref[...])
    gates_ref[0] = _dot(hn, wgate_ref[...]).astype(gates_ref.dtype)


def _fox_kernel(qT_ref, k_ref, vT_ref, o_ref, s_buf, p_buf):
    i = pl.program_id(2)
    qTs = [qT_ref[0, hh, 0] for hh in range(NHA)]
    nrc = TK // RC
    chunk = lambda r: slice(r * RC, (r + 1) * RC)

    def score_chunk(j, hh, r):
        off = pl.multiple_of(j * TK + r * RC, RC)
        return _dot(k_ref[0, hh, pl.ds(off, RC), :], qTs[hh])

    cmax0 = []
    for hh in range(NHA):
        cm = jnp.full((1, TQ), NEG, F32)
        for r in range(nrc):
            sc = score_chunk(0, hh, r)
            cm = jnp.maximum(cm, jnp.max(sc, axis=0, keepdims=True))
            s_buf[hh, chunk(r), :] = sc
        cmax0.append(cm)
    p_buf[...] = jnp.zeros_like(p_buf)

    def body(j, st):
        jp = jnp.maximum(j - 1, 0)
        m_new = [jnp.maximum(st[hh][0], st[hh][3]) for hh in range(NHA)]
        alpha = [jnp.exp2(st[hh][0] - m_new[hh]) for hh in range(NHA)]
        accs = [st[hh][2] * st[hh][1] for hh in range(NHA)]
        cm_next = [jnp.full((1, TQ), NEG, F32) for _ in range(NHA)]
        for r in range(nrc):
            for hh in range(NHA):
                accs[hh] = accs[hh] + _dot(vT_ref[0, hh, jp, :, chunk(r)], p_buf[hh, chunk(r), :])
                p_buf[hh, chunk(r), :] = jnp.exp2(s_buf[hh, chunk(r), :] - m_new[hh]).astype(BF16)
                sc = score_chunk(j + 1, hh, r)
                cm_next[hh] = jnp.maximum(cm_next[hh], jnp.max(sc, axis=0, keepdims=True))
                s_buf[hh, chunk(r), :] = sc
        return tuple((m_new[hh], accs[hh], alpha[hh], cm_next[hh]) for hh in range(NHA))

    init = tuple((jnp.full((1, TQ), NEG, F32), jnp.zeros((VROWS, TQ), F32),
                  jnp.ones((1, TQ), F32), cmax0[hh]) for hh in range(NHA))
    st = lax.fori_loop(0, i, body, init)

    causal = (lax.broadcasted_iota(jnp.int32, (TK, TQ), 0)
              <= lax.broadcasted_iota(jnp.int32, (TK, TQ), 1))
    ip = jnp.maximum(i - 1, 0)
    outs = []
    for hh in range(NHA):
        m, acc, a_prev, _ = st[hh]
        acc = a_prev * acc + _dot(vT_ref[0, hh, ip], p_buf[hh])
        sT = jnp.where(causal, s_buf[hh], NEG)
        m_new = jnp.maximum(m, jnp.max(sT, axis=0, keepdims=True))
        acc = jnp.exp2(m - m_new) * acc + _dot(vT_ref[0, hh, i], jnp.exp2(sT - m_new).astype(BF16))
        outs.append(acc[:HEAD_DIM] / acc[HEAD_DIM:HEAD_DIM + 1])
    oT = jnp.concatenate(outs, axis=0)
    o_ref[0] = oT.T.astype(o_ref.dtype)


def _nt(a, b):
    return lax.dot_general(a, b, (((1,), (1,)), ((), ())), preferred_element_type=F32)


def _tn(a, b):
    return lax.dot_general(a, b, (((0,), (0,)), ((), ())), preferred_element_type=F32)


def _gdn_kernel(gq_ref, small_ref, z_ref, convw_ref, gbias_ref, alog_ref, normw_ref,
                tril_ref, last_ref, eb_ref, eg_ref, bd_ref, bdmask_ref, tile_ref, hsel_ref,
                o_ref, convbuf, state, obuf):
    i = pl.program_id(0)
    nb = gq_ref.shape[0]

    @pl.when(i == 0)
    def _():
        convbuf[:, 0:8, :] = jnp.zeros((nb, 8, 3 * WIDTH), F32)
        state[...] = jnp.zeros_like(state)

    w = convw_ref[...]
    bd = bd_ref[...]

    def prepare(b):
        convbuf[b, 8:8 + GROWS, :] = gq_ref[b]
        xc = w[GDN_CONV - 1:GDN_CONV, :] * convbuf[b, 8:8 + GROWS, :]
        for t in range(1, GDN_CONV):
            xc = xc + w[GDN_CONV - 1 - t:GDN_CONV - t, :] * convbuf[b, 8 - t:8 - t + GROWS, :]
        convbuf[b, 0:8, :] = convbuf[b, GROWS:GROWS + 8, :]
        xs = _silu(xc)
        q = xs[:, :WIDTH]
        k = xs[:, WIDTH:2 * WIDTH]
        v = xs[:, 2 * WIDTH:]
        qn = q * lax.rsqrt(_dot_split_lhs(q * q, bd, 2) + RMS_EPS) * (HEAD_DIM ** -0.5)
        kn = k * lax.rsqrt(_dot_split_lhs(k * k, bd, 2) + RMS_EPS)
        small = small_ref[b]
        beta_e = _dot_split_lhs(jax.nn.sigmoid(small), eb_ref[...], 2)
        g = -jnp.exp(alog_ref[...]) * _softplus(small + gbias_ref[...])
        gc = _dot_split_rhs(tril_ref[...], g, 3)
        gl = _dot_split_rhs(last_ref[...], gc, 3)
        gc_e = _dot_split_lhs(gc, eg_ref[...], 3)
        gl_e = _dot_split_lhs(gl, eg_ref[...], 3)
        gcT = gc.T
        eg = jnp.exp(gc_e)
        kb = kn * beta_e
        gsel = _dot_split_lhs(gcT[L_A:L_A + HEADS, :], tile_ref[...], 3) * hsel_ref[...]
        return dict(qn=qn, kn=kn, kb=kb, kbe=kb * eg, kt=kn * jnp.exp(gl_e - gc_e), vb=v * beta_e,
                    qd=qn * eg, gc_e=gc_e, cd_e=jnp.exp(gl_e),
                    grow=jnp.sum(gsel, axis=0, keepdims=True))

    pre = [prepare(b) for b in range(nb)]

    nch = GROWS // CHUNK
    nquad = HEADS // QH
    rowi = lax.broadcasted_iota(jnp.int32, (CHUNK, QW), 0)
    posi = lax.broadcasted_iota(jnp.int32, (CHUNK, QW), 1) % HEAD_DIM
    tril = rowi >= posi
    strict = rowi > posi
    eye = (rowi == posi).astype(F32)
    bdm = bdmask_ref[...]
    bdm16 = bdm.astype(BF16)

    def bd4(x):
        return jnp.concatenate([x.astype(BF16)] * QH, axis=0) * bdm16

    def bdot(lhs, x):
        return _dot(lhs.astype(BF16), bd4(x))

    items = [(b, c, u) for b in range(nb) for c in range(nch) for u in range(nquad)]
    rows_of = lambda c: slice(c * CHUNK, (c + 1) * CHUNK)
    cols_of = lambda u: slice(u * QW, (u + 1) * QW)
    blk = lambda name, b, c, u: pre[b][name][rows_of(c), cols_of(u)]
    a_l, attn_l = {}, {}
    for (b, c, u) in items:
        grow = pre[b]["grow"][:, c * WIDTH + u * QW:c * WIDTH + (u + 1) * QW]
        decay = jnp.where(tril, jnp.exp(jnp.where(tril, blk("gc_e", b, c, u) - grow, 0.0)), 0.0)
        aa = _nt(jnp.concatenate([blk("kb", b, c, u), blk("qn", b, c, u)], axis=0).astype(BF16),
                 bd4(blk("kn", b, c, u)))
        a_l[b, c, u] = jnp.where(strict, aa[:CHUNK] * decay, 0.0)
        attn_l[b, c, u] = aa[CHUNK:] * decay

    t_l = {it: eye - a_l[it] for it in items}
    p_l = {it: bdot(a_l[it], a_l[it]) for it in items}
    steps = int(math.log2(CHUNK)) - 1
    for s in range(steps):
        for it in items:
            if s + 1 < steps:
                r = bdot(jnp.concatenate([t_l[it], p_l[it]], axis=0), p_l[it])
                t_l[it] = t_l[it] + r[:CHUNK]
                p_l[it] = r[CHUNK:]
            else:
                t_l[it] = t_l[it] + bdot(t_l[it], p_l[it])

    val_l = {it: bdot(t_l[it], blk("vb", *it)) for it in items}
    kcd_l = {it: bdot(t_l[it], blk("kbe", *it)) for it in items}

    lanes = [(b, u) for b in range(nb) for u in range(nquad)]
    s_cur = {bu: state[bu[0], bu[1]] for bu in lanes}
    for c in range(nch):
        kq = {(b, u): _dot(jnp.concatenate([kcd_l[b, c, u], blk("qd", b, c, u)], axis=0), s_cur[b, u])
              for (b, u) in lanes}
        v_new = {(b, u): val_l[b, c, u] - kq[b, u][:CHUNK] for (b, u) in lanes}
        o_l = {(b, u): kq[b, u][CHUNK:] + bdot(attn_l[b, c, u], v_new[b, u]) for (b, u) in lanes}
        upd = {(b, u): _tn(blk("kt", b, c, u), v_new[b, u]) for (b, u) in lanes}
        for (b, u) in lanes:
            cd = pre[b]["cd_e"][(c + 1) * CHUNK - 1:(c + 1) * CHUNK, cols_of(u)]
            s_cur[b, u] = (s_cur[b, u] * cd + upd[b, u]) * bdm
            obuf[b, rows_of(c), cols_of(u)] = o_l[b, u]
    for (b, u) in lanes:
        state[b, u] = s_cur[b, u]

    for b in range(nb):
        o = obuf[b]
        ms = _dot_split_lhs(o * o, bd, 2) * (1.0 / HEAD_DIM)
        o_ref[b] = (o * lax.rsqrt(ms + RMS_EPS) * normw_ref[...] * _silu(z_ref[b])).astype(o_ref.dtype)


def _merge_ffn_kernel(x_ref, meta_ref, ofox_ref, ogdn_ref, gates_ref, gbias_ref, wbf_ref, wbg_ref,
                      wout_ref, nw_ref, wup_ref, convw_ref, convb_ref, wdown_ref, fw_ref, o_ref,
                      ubuf, tail, act):
    i = pl.program_id(1)

    @pl.when(i == 0)
    def _():
        tail[...] = jnp.zeros_like(tail)

    h = jnp.where(i < NF, meta_ref[...], x_ref[0])
    y_fox = _dot(ofox_ref[0], wbf_ref[...])
    y_gdn = _dot(ogdn_ref[0], wbg_ref[...])
    gate = jax.nn.sigmoid(gates_ref[0].astype(F32) + gbias_ref[...])
    mix = gate[:, :D_MODEL] * y_fox + gate[:, D_MODEL:] * y_gdn
    h2 = h + _dot(mix.astype(BF16), wout_ref[...])
    row = lax.broadcasted_iota(jnp.int32, h2.shape, 0)
    h2 = jnp.where(i * TM + row < PAD_ROWS, 0.0, h2)
    hn = _rms(h2, nw_ref[...]).astype(BF16)

    def conv_cols(c0, width):
        ubuf[0:8, 0:width] = tail[:, c0:c0 + width]
        ubuf[8:8 + TM, 0:width] = _dot(hn, wup_ref[:, c0:c0 + width])
        tail[:, c0:c0 + width] = ubuf[TM:TM + 8, 0:width]
        w = convw_ref[:, c0:c0 + width]
        y = convb_ref[:, c0:c0 + width] + w[FFN_CONV - 1:FFN_CONV, :] * ubuf[8:8 + TM, 0:width]
        for t in range(1, FFN_CONV):
            y = y + w[FFN_CONV - 1 - t:FFN_CONV - t, :] * ubuf[8 - t:8 - t + TM, 0:width]
        return y

    for n in range(D_FF // FF_CHUNK):
        c0 = n * FF_CHUNK
        gate_u = conv_cols(c0, FF_CHUNK)
        up = conv_cols(D_FF + c0, FF_CHUNK)
        act[:, c0:c0 + FF_CHUNK] = (_silu(gate_u) * up).astype(BF16)

    h3 = h2 + _dot(act[...], wdown_ref[...])
    o_ref[0] = _rms(h3, fw_ref[...])


def _constants():
    qa = LANES - HEAD_DIM
    placeq = np.zeros((LANES, HEADS * qa), np.float32)
    placek = np.zeros((LANES, HEADS * LANES), np.float32)
    onesq = np.zeros((1, HEADS * qa), np.float32)
    onesk = np.zeros((1, HEADS * LANES), np.float32)
    for hd in range(HEADS):
        for part in range(3):
            placeq[part * HEADS + hd, hd * qa + A_QC - HEAD_DIM + part] = 1.0
            placek[part * HEADS + hd, hd * LANES + A_KC + part] = -1.0
        onesq[0, hd * qa + A_KC - HEAD_DIM:hd * qa + A_PAD - HEAD_DIM + 1] = 1.0
        onesk[0, hd * LANES + A_QC:hd * LANES + A_QC + 3] = 1.0
    tril_tm = np.tril(np.ones((TM, TM), np.float32))
    r = np.arange(GROWS)
    same = (r[:, None] // CHUNK) == (r[None, :] // CHUNK)
    tril_g = (same & (r[:, None] >= r[None, :])).astype(np.float32)
    last_g = (r[None, :] == (r[:, None] // CHUNK) * CHUNK + CHUNK - 1).astype(np.float32)
    eb = np.zeros((LANES, WIDTH), np.float32)
    eg = np.zeros((LANES, WIDTH), np.float32)
    for hd in range(HEADS):
        eb[L_B + hd, hd * HEAD_DIM:(hd + 1) * HEAD_DIM] = 1.0
        eg[L_A + hd, hd * HEAD_DIM:(hd + 1) * HEAD_DIM] = 1.0
    cw = np.arange(WIDTH)
    bd = ((cw[:, None] // HEAD_DIM) == (cw[None, :] // HEAD_DIM)).astype(np.float32)
    c2 = np.arange(QW)
    bdmask = ((c2[:, None] // HEAD_DIM) == (c2[None, :] // HEAD_DIM)).astype(np.float32)
    col = np.arange((GROWS // CHUNK) * WIDTH)
    tile_g = ((r[:, None] // CHUNK == col[None, :] // WIDTH)
              & (r[:, None] % CHUNK == col[None, :] % HEAD_DIM)).astype(np.float32)
    hsel = (np.arange(HEADS)[:, None] == (col[None, :] % WIDTH) // HEAD_DIM).astype(np.float32)
    return dict(placeq=placeq, placek=placek, onesq=onesq, onesk=onesk, tril_tm=tril_tm,
                tril_g=tril_g, last_g=last_g, eb=eb, eg=eg, bd=bd, bdmask=bdmask,
                tile_g=tile_g, hsel=hsel)


def _spread_heads(w):
    d = w.shape[0]
    w = w.reshape(d, HEADS, HEAD_DIM)
    return jnp.pad(w, ((0, 0), (0, 0), (0, LANES - HEAD_DIM))).reshape(d, HEADS * LANES)


def _lane_row(vals, start, copies=1):
    out = jnp.zeros((1, LANES), F32)
    for c in range(copies):
        out = lax.dynamic_update_slice(out, vals.reshape(1, -1).astype(F32), (0, start + c * vals.shape[-1]))
    return out


def kernel(x, meta_tokens, w_in, fgt_bias, gdn_conv_w, gdn_a_log, gdn_dt_bias, gdn_norm_w, gate_bias,
           w_branch_fox, w_branch_gdn, w_out, norm_mix_w, norm_ffn_w, ffn_w_up, ffn_conv_w,
           ffn_conv_b, ffn_w_down, norm_final_w):
    bn, seq, d = x.shape
    assert d == D_MODEL and seq % TM == 0 and w_in.shape[0] == 1
    p_rows = seq + FRONT
    nt = p_rows // TM
    cst = _constants()

    w = w_in[0]
    o = 0
    wq, wk, wv = (w[:, o + j * WIDTH:o + (j + 1) * WIDTH] for j in range(3))
    o += 3 * WIDTH
    wf = w[:, o:o + HEADS]
    o += HEADS
    wg = w[:, o:o + 3 * WIDTH]
    o += 3 * WIDTH
    wz = w[:, o:o + WIDTH]
    o += WIDTH
    wb = w[:, o:o + HEADS]
    o += HEADS
    wa = w[:, o:o + HEADS]
    o += HEADS
    wgate = w[:, o:]
    ws = jnp.concatenate([wf, wf, wf, wb, wa, jnp.zeros((d, LANES - 5 * HEADS), F32)], axis=1)
    meta_tile = jnp.concatenate([jnp.zeros((PAD_ROWS, d), F32), meta_tokens.astype(F32)], axis=0)
    sbias = _lane_row(fgt_bias[0], L_F, copies=3)
    gbias = _lane_row(gdn_dt_bias[0], L_A)
    alog = _lane_row(gdn_a_log[0], L_A)
    row = lambda a: a.reshape(1, -1).astype(F32)
    b16 = lambda a: jnp.asarray(a).astype(BF16)

    cparams = lambda sem: pltpu.CompilerParams(dimension_semantics=sem, vmem_limit_bytes=VMEM_LIMIT)
    x_spec = pl.BlockSpec((1, TM, d), lambda b, i: (b, jnp.maximum(i - NF, 0), 0))
    meta_spec = pl.BlockSpec((TM, d), lambda b, i: (jnp.minimum(i, NF - 1), 0))
    tile = lambda width: pl.BlockSpec((1, TM, width), lambda b, i: (b, i, 0))

    k1_in = [x, meta_tile, row(norm_mix_w[0]), b16(wq), b16(_spread_heads(wk)), b16(wv),
             b16(ws), b16(wg), b16(wz), b16(wgate), sbias, b16(cst["placeq"]), b16(cst["placek"]),
             jnp.asarray(cst["onesq"]), jnp.asarray(cst["onesk"]), b16(cst["tril_tm"])]
    k1_specs = [x_spec, meta_spec] + [_const_spec(a.shape) for a in k1_in[2:]]
    qT, kaug, vT, gqkv, z, small, gates = pl.pallas_call(
        _inproj_kernel,
        grid=(bn, nt),
        in_specs=k1_specs,
        out_specs=[
            pl.BlockSpec((1, HEADS, 1, LANES, TM), lambda b, i: (b, 0, i // TPA, 0, i % TPA)),
            pl.BlockSpec((1, HEADS, TM, LANES), lambda b, i: (b, 0, i, 0)),
            pl.BlockSpec((1, HEADS, 1, VROWS, TM), lambda b, i: (b, 0, i // TPA, 0, i % TPA)),
            tile(3 * WIDTH), tile(WIDTH), tile(LANES), tile(2 * D_MODEL)],
        out_shape=[
            jax.ShapeDtypeStruct((bn, HEADS, p_rows // TQ, LANES, TQ), BF16),
            jax.ShapeDtypeStruct((bn, HEADS, p_rows, LANES), BF16),
            jax.ShapeDtypeStruct((bn, HEADS, p_rows // TK, VROWS, TK), BF16),
            jax.ShapeDtypeStruct((bn, p_rows, 3 * WIDTH), F32),
            jax.ShapeDtypeStruct((bn, p_rows, WIDTH), F32),
            jax.ShapeDtypeStruct((bn, p_rows, LANES), F32),
            jax.ShapeDtypeStruct((bn, p_rows, 2 * D_MODEL), BF16)],
        scratch_shapes=[pltpu.VMEM((8, LANES), F32)],
        compiler_params=cparams(("arbitrary", "arbitrary")),
        name="inproj",
    )(*k1_in)

    nq = p_rows // TQ
    o_fox = pl.pallas_call(
        _fox_kernel,
        grid=(bn, HEADS // NHA, nq),
        in_specs=[
            pl.BlockSpec((1, NHA, 1, LANES, TQ), lambda b, p, i: (b, p, i, 0, 0)),
            pl.BlockSpec((1, NHA, p_rows, LANES), lambda b, p, i: (b, p, 0, 0),
                         pipeline_mode=pl.Buffered(1)),
            pl.BlockSpec((1, NHA, p_rows // TK, VROWS, TK), lambda b, p, i: (b, p, 0, 0, 0),
                         pipeline_mode=pl.Buffered(1))],
        out_specs=pl.BlockSpec((1, TQ, NHA * HEAD_DIM), lambda b, p, i: (b, i, p)),
        out_shape=jax.ShapeDtypeStruct((bn, p_rows, WIDTH), BF16),
        scratch_shapes=[pltpu.VMEM((NHA, TK, TQ), F32), pltpu.VMEM((NHA, TK, TQ), BF16)],
        compiler_params=cparams(("arbitrary", "arbitrary", "arbitrary")),
        name="fox_attention",
    )(qT, kaug, vT)

    gtile = lambda width: pl.BlockSpec((bn, GROWS, width), lambda i: (0, i, 0))
    k3_const = [gdn_conv_w[0].astype(F32), gbias, alog, row(jnp.tile(gdn_norm_w[0], HEADS)),
                b16(cst["tril_g"]), b16(cst["last_g"]), b16(cst["eb"]), b16(cst["eg"]), b16(cst["bd"]),
                jnp.asarray(cst["bdmask"]), b16(cst["tile_g"]), jnp.asarray(cst["hsel"])]
    o_gdn = pl.pallas_call(
        _gdn_kernel,
        grid=(p_rows // GROWS,),
        in_specs=[gtile(3 * WIDTH), gtile(LANES), gtile(WIDTH)] + [_const_spec(a.shape) for a in k3_const],
        out_specs=gtile(WIDTH),
        out_shape=jax.ShapeDtypeStruct((bn, p_rows, WIDTH), BF16),
        scratch_shapes=[pltpu.VMEM((bn, GROWS + 8, 3 * WIDTH), F32),
                        pltpu.VMEM((bn, HEADS // QH, QW, QW), F32),
                        pltpu.VMEM((bn, GROWS, WIDTH), F32)],
        compiler_params=cparams(("arbitrary",)),
        name="gated_deltanet",
    )(gqkv, small, z, *k3_const)

    k4_const = [row(gate_bias[0]), b16(w_branch_fox[0]), b16(w_branch_gdn[0]), b16(w_out[0]),
                row(norm_ffn_w[0]), b16(ffn_w_up[0]), ffn_conv_w[0].astype(F32), row(ffn_conv_b[0]),
                b16(ffn_w_down[0]), row(norm_final_w)]
    out = pl.pallas_call(
        _merge_ffn_kernel,
        grid=(bn, nt),
        in_specs=[x_spec, meta_spec, tile(WIDTH), tile(WIDTH), tile(2 * D_MODEL)]
        + [_const_spec(a.shape) for a in k4_const],
        out_specs=pl.BlockSpec((1, TM, d), lambda b, i: (b, jnp.maximum(i - NF, 0), 0)),
        out_shape=jax.ShapeDtypeStruct((bn, seq, d), F32),
        scratch_shapes=[pltpu.VMEM((TM + 8, FF_CHUNK), F32),
                        pltpu.VMEM((8, 2 * D_FF), F32),
                        pltpu.VMEM((TM, D_FF), BF16)],
        compiler_params=cparams(("arbitrary", "arbitrary")),
        name="merge_ffn",
    )(x, meta_tile, o_fox, o_gdn, gates, *k4_const)
    return out
```

```python
import functools
import math

import jax
import jax.numpy as jnp
import numpy as np
from jax import lax
from jax.experimental import pallas as pl
from jax.experimental.pallas import tpu as pltpu

F32 = jnp.float32
BF16 = jnp.bfloat16

D_MODEL = 1024
N_META = 16
HEADS = 8
HEAD_DIM = 64
WIDTH = HEADS * HEAD_DIM
GDN_CONV = 4
D_FF = 2816
FFN_CONV = 3
CHUNK = 64
RMS_EPS = 1e-6
NEG = -1e30
LOG2E = math.log2(math.e)

LANES = 128
FRONT = 512
PAD_ROWS = FRONT - N_META
TM = 256
NF = FRONT // TM
TMF = 512
NFF = FRONT // TMF
TQ = 512
TK = 512
TPA = TQ // TM
RC = 256
NHA = 4
VROWS = HEAD_DIM + 16
GROWS = 4 * CHUNK
QH = 4
QW = QH * HEAD_DIM
FF_CHUNK = 256
VMEM_LIMIT = 56 * 1024 * 1024

L_F = 0
L_B = 24
L_A = 32
A_QC = 64
A_KC = 67
A_PAD = 70


def _const_spec(shape):
    n = len(shape)
    return pl.BlockSpec(shape, lambda *_: (0,) * n, pipeline_mode=pl.Buffered(1))


def _split3(x):
    hi = x.astype(BF16)
    r = x - hi.astype(F32)
    mid = r.astype(BF16)
    lo = (r - mid.astype(F32)).astype(BF16)
    return hi, mid, lo


def _dot(a, b):
    return jnp.dot(a, b, preferred_element_type=F32)


def _dot_split_lhs(x, m, n):
    parts = _split3(x)[:n]
    out = _dot(parts[0], m)
    for p in parts[1:]:
        out = out + _dot(p, m)
    return out


def _dot_split_rhs(m, x, n):
    parts = _split3(x)[:n]
    out = _dot(m, parts[0])
    for p in parts[1:]:
        out = out + _dot(m, p)
    return out


def _rms(h, w):
    ms = jnp.mean(h * h, axis=-1, keepdims=True)
    return h * lax.rsqrt(ms + RMS_EPS) * w


def _log_sigmoid(x):
    return jnp.minimum(x, 0.0) - jnp.log1p(jnp.exp(-jnp.abs(x)))


def _softplus(x):
    return jnp.maximum(x, 0.0) + jnp.log1p(jnp.exp(-jnp.abs(x)))


def _silu(x):
    return x * jax.nn.sigmoid(x)


def _inproj_kernel(x_ref, meta_ref, nw_ref, wq_ref, wk_ref, wv_ref, ws_ref, wg_ref, wz_ref,
                   wgate_ref, sbias_ref, placeq_ref, placek_ref, onesq_ref, onesk_ref, tril_ref,
                   qT_ref, kaug_ref, vT_ref, gqkv_ref, z_ref, small_ref, gates_ref, carry_ref):
    i = pl.program_id(1)
    h = jnp.where(i < NF, meta_ref[...], x_ref[0])
    hn = _rms(h, nw_ref[...]).astype(BF16)

    small = _dot(hn, ws_ref[...])
    small_ref[0] = small

    @pl.when(i == 0)
    def _():
        carry_ref[...] = jnp.zeros_like(carry_ref)

    lf = _log_sigmoid(small + sbias_ref[...]) * LOG2E
    c = _dot_split_rhs(tril_ref[...], lf, 3) + carry_ref[0:1, :]
    carry_ref[...] = jnp.broadcast_to(c[TM - 1:TM, :], carry_ref.shape)
    hi = c.astype(BF16).astype(F32)
    r = c - hi
    mid = r.astype(BF16).astype(F32)
    lo = r - mid
    lane = lax.broadcasted_iota(jnp.int32, c.shape, 1)
    parts = jnp.where(lane < HEADS, hi, jnp.where(lane < 2 * HEADS, mid, lo)).astype(BF16)

    kaug = _dot(hn, wk_ref[...]) + _dot(parts, placek_ref[...]) + onesk_ref[...]
    row = lax.broadcasted_iota(jnp.int32, kaug.shape, 0)
    col = lax.broadcasted_iota(jnp.int32, kaug.shape, 1)
    inert = jnp.logical_and(i * TM + row < PAD_ROWS, (col % LANES) == A_PAD)
    kaug = jnp.where(inert, NEG, kaug).astype(BF16)
    for hd in range(HEADS):
        kaug_ref[0, hd] = kaug[:, hd * LANES:(hd + 1) * LANES]

    qT = (_dot(hn, wq_ref[...]) * (HEAD_DIM ** -0.5 * LOG2E)).T
    qT_ref[0, :, 0, 0:HEAD_DIM, :] = qT.reshape(HEADS, HEAD_DIM, TM).astype(BF16)
    augT = (_dot(parts, placeq_ref[...]) + onesq_ref[...]).T
    qT_ref[0, :, 0, HEAD_DIM:LANES, :] = augT.reshape(HEADS, LANES - HEAD_DIM, TM).astype(BF16)

    vT = _dot(hn, wv_ref[...]).T
    vT_ref[0, :, 0, 0:HEAD_DIM, :] = vT.reshape(HEADS, HEAD_DIM, TM).astype(BF16)
    ones_row = lax.broadcasted_iota(jnp.int32, (HEADS, VROWS - HEAD_DIM, TM), 1) == 0
    vT_ref[0, :, 0, HEAD_DIM:VROWS, :] = ones_row.astype(F32).astype(BF16)

    gqkv_ref[0] = _dot(hn, wg_ref[...])
    z_ref[0] = _dot(hn, wz_ref[...])
    gates_ref[0] = _dot(hn, wgate_ref[...]).astype(gates_ref.dtype)


def _fox_kernel(qT_ref, k_ref, vT_ref, o_ref, s_buf, p_buf):
    i = pl.program_id(2)
    qTs = [qT_ref[0, hh, 0] for hh in range(NHA)]
    nrc = TK // RC
    chunk = lambda r: slice(r * RC, (r + 1) * RC)

    def score_chunk(j, hh, r):
        off = pl.multiple_of(j * TK + r * RC, RC)
        return _dot(k_ref[0, hh, pl.ds(off, RC), :], qTs[hh])

    cmax0 = []
    for hh in range(NHA):
        cm = jnp.full((1, TQ), NEG, F32)
        for r in range(nrc):
            sc = score_chunk(0, hh, r)
            cm = jnp.maximum(cm, jnp.max(sc, axis=0, keepdims=True))
            s_buf[hh, chunk(r), :] = sc
        cmax0.append(cm)
    p_buf[...] = jnp.zeros_like(p_buf)

    def body(j, st):
        jp = jnp.maximum(j - 1, 0)
        m_new = [jnp.maximum(st[hh][0], st[hh][3]) for hh in range(NHA)]
        alpha = [jnp.exp2(st[hh][0] - m_new[hh]) for hh in range(NHA)]
        accs = [st[hh][2] * st[hh][1] for hh in range(NHA)]
        cm_next = [jnp.full((1, TQ), NEG, F32) for _ in range(NHA)]
        for r in range(nrc):
            for hh in range(NHA):
                accs[hh] = accs[hh] + _dot(vT_ref[0, hh, jp, :, chunk(r)], p_buf[hh, chunk(r), :])
                p_buf[hh, chunk(r), :] = jnp.exp2(s_buf[hh, chunk(r), :] - m_new[hh]).astype(BF16)
                sc = score_chunk(j + 1, hh, r)
                cm_next[hh] = jnp.maximum(cm_next[hh], jnp.max(sc, axis=0, keepdims=True))
                s_buf[hh, chunk(r), :] = sc
        return tuple((m_new[hh], accs[hh], alpha[hh], cm_next[hh]) for hh in range(NHA))

    init = tuple((jnp.full((1, TQ), NEG, F32), jnp.zeros((VROWS, TQ), F32),
                  jnp.ones((1, TQ), F32), cmax0[hh]) for hh in range(NHA))
    st = lax.fori_loop(0, i // 2, lambda jj, t: body(2 * jj + 1, body(2 * jj, t)), init)
    st = lax.cond(i % 2 == 1, lambda t: body(i - 1, t), lambda t: t, st)

    causal = (lax.broadcasted_iota(jnp.int32, (TK, TQ), 0)
              <= lax.broadcasted_iota(jnp.int32, (TK, TQ), 1))
    ip = jnp.maximum(i - 1, 0)
    outs = []
    for hh in range(NHA):
        m, acc, a_prev, _ = st[hh]
        acc = a_prev * acc + _dot(vT_ref[0, hh, ip], p_buf[hh])
        sT = jnp.where(causal, s_buf[hh], NEG)
        m_new = jnp.maximum(m, jnp.max(sT, axis=0, keepdims=True))
        acc = jnp.exp2(m - m_new) * acc + _dot(vT_ref[0, hh, i], jnp.exp2(sT - m_new).astype(BF16))
        outs.append(acc[:HEAD_DIM] / acc[HEAD_DIM:HEAD_DIM + 1])
    oT = jnp.concatenate(outs, axis=0)
    o_ref[0] = oT.T.astype(o_ref.dtype)


def _nt(a, b):
    return lax.dot_general(a, b, (((1,), (1,)), ((), ())), preferred_element_type=F32)


def _tn(a, b):
    return lax.dot_general(a, b, (((0,), (0,)), ((), ())), preferred_element_type=F32)


def _gdn_kernel(gq_ref, small_ref, z_ref, convw_ref, gbias_ref, alog_ref, normw_ref,
                tril_ref, last_ref, eb_ref, eg_ref, bd_ref, bdmask_ref, tile_ref, hsel_ref,
                o_ref, convbuf, state, obuf):
    i = pl.program_id(0)
    nb = gq_ref.shape[0]

    @pl.when(i == 0)
    def _():
        convbuf[:, 0:8, :] = jnp.zeros((nb, 8, 3 * WIDTH), F32)
        state[...] = jnp.zeros_like(state)

    w = convw_ref[...]
    bd = bd_ref[...]

    def prepare(b):
        convbuf[b, 8:8 + GROWS, :] = gq_ref[b]
        xc = w[GDN_CONV - 1:GDN_CONV, :] * convbuf[b, 8:8 + GROWS, :]
        for t in range(1, GDN_CONV):
            xc = xc + w[GDN_CONV - 1 - t:GDN_CONV - t, :] * convbuf[b, 8 - t:8 - t + GROWS, :]
        convbuf[b, 0:8, :] = convbuf[b, GROWS:GROWS + 8, :]
        xs = _silu(xc)
        q = xs[:, :WIDTH]
        k = xs[:, WIDTH:2 * WIDTH]
        v = xs[:, 2 * WIDTH:]
        qn = q * lax.rsqrt(_dot_split_lhs(q * q, bd, 2) + RMS_EPS) * (HEAD_DIM ** -0.5)
        kn = k * lax.rsqrt(_dot_split_lhs(k * k, bd, 2) + RMS_EPS)
        small = small_ref[b]
        beta_e = _dot_split_lhs(jax.nn.sigmoid(small), eb_ref[...], 2)
        g = -jnp.exp(alog_ref[...]) * _softplus(small + gbias_ref[...])
        gc = _dot_split_rhs(tril_ref[...], g, 3)
        gl = _dot_split_rhs(last_ref[...], gc, 3)
        gc_e = _dot_split_lhs(gc, eg_ref[...], 3)
        gl_e = _dot_split_lhs(gl, eg_ref[...], 3)
        gcT = gc.T
        eg = jnp.exp(gc_e)
        kb = kn * beta_e
        gsel = _dot_split_lhs(gcT[L_A:L_A + HEADS, :], tile_ref[...], 3) * hsel_ref[...]
        return dict(qn=qn, kn=kn, kb=kb, kbe=kb * eg, kt=kn * jnp.exp(gl_e - gc_e), vb=v * beta_e,
                    qd=qn * eg, gc_e=gc_e, cd_e=jnp.exp(gl_e),
                    grow=jnp.sum(gsel, axis=0, keepdims=True))

    pre = [prepare(b) for b in range(nb)]

    nch = GROWS // CHUNK
    nquad = HEADS // QH
    rowi = lax.broadcasted_iota(jnp.int32, (CHUNK, QW), 0)
    posi = lax.broadcasted_iota(jnp.int32, (CHUNK, QW), 1) % HEAD_DIM
    tril = rowi >= posi
    strict = rowi > posi
    eye = (rowi == posi).astype(F32)
    bdm = bdmask_ref[...]
    bdm16 = bdm.astype(BF16)

    def bd4(x):
        return jnp.concatenate([x.astype(BF16)] * QH, axis=0) * bdm16

    def bdot(lhs, x):
        return _dot(lhs.astype(BF16), bd4(x))

    items = [(b, c, u) for b in range(nb) for c in range(nch) for u in range(nquad)]
    rows_of = lambda c: slice(c * CHUNK, (c + 1) * CHUNK)
    cols_of = lambda u: slice(u * QW, (u + 1) * QW)
    blk = lambda name, b, c, u: pre[b][name][rows_of(c), cols_of(u)]
    a_l, attn_l = {}, {}
    for (b, c, u) in items:
        grow = pre[b]["grow"][:, c * WIDTH + u * QW:c * WIDTH + (u + 1) * QW]
        decay = jnp.where(tril, jnp.exp(jnp.where(tril, blk("gc_e", b, c, u) - grow, 0.0)), 0.0)
        aa = _nt(jnp.concatenate([blk("kb", b, c, u), blk("qn", b, c, u)], axis=0).astype(BF16),
                 bd4(blk("kn", b, c, u)))
        a_l[b, c, u] = jnp.where(strict, aa[:CHUNK] * decay, 0.0)
        attn_l[b, c, u] = aa[CHUNK:] * decay

    t_l = {it: eye - a_l[it] for it in items}
    p_l = {it: bdot(a_l[it], a_l[it]) for it in items}
    steps = int(math.log2(CHUNK)) - 1
    for s in range(steps):
        for it in items:
            if s + 1 < steps:
                r = bdot(jnp.concatenate([t_l[it], p_l[it]], axis=0), p_l[it])
                t_l[it] = t_l[it] + r[:CHUNK]
                p_l[it] = r[CHUNK:]
            else:
                t_l[it] = t_l[it] + bdot(t_l[it], p_l[it])

    val_l = {it: bdot(t_l[it], blk("vb", *it)) for it in items}
    kcd_l = {it: bdot(t_l[it], blk("kbe", *it)) for it in items}

    lanes = [(b, u) for b in range(nb) for u in range(nquad)]
    s_cur = {bu: state[bu[0], bu[1]] for bu in lanes}
    for c in range(nch):
        kq = {(b, u): _dot(jnp.concatenate([kcd_l[b, c, u], blk("qd", b, c, u)], axis=0), s_cur[b, u])
              for (b, u) in lanes}
        v_new = {(b, u): val_l[b, c, u] - kq[b, u][:CHUNK] for (b, u) in lanes}
        o_l = {(b, u): kq[b, u][CHUNK:] + bdot(attn_l[b, c, u], v_new[b, u]) for (b, u) in lanes}
        upd = {(b, u): _tn(blk("kt", b, c, u), v_new[b, u]) for (b, u) in lanes}
        for (b, u) in lanes:
            cd = pre[b]["cd_e"][(c + 1) * CHUNK - 1:(c + 1) * CHUNK, cols_of(u)]
            s_cur[b, u] = (s_cur[b, u] * cd + upd[b, u]) * bdm
            obuf[b, rows_of(c), cols_of(u)] = o_l[b, u]
    for (b, u) in lanes:
        state[b, u] = s_cur[b, u]

    for b in range(nb):
        o = obuf[b]
        ms = _dot_split_lhs(o * o, bd, 2) * (1.0 / HEAD_DIM)
        o_ref[b] = (o * lax.rsqrt(ms + RMS_EPS) * normw_ref[...] * _silu(z_ref[b])).astype(o_ref.dtype)


def _merge_ffn_kernel(x_ref, meta_ref, ofox_ref, ogdn_ref, gates_ref, gbias_ref, wbf_ref, wbg_ref,
                      wout_ref, nw_ref, wup_ref, convw_ref, convb_ref, wdown_ref, fw_ref, o_ref,
                      ubuf, tail, act):
    i = pl.program_id(1)

    @pl.when(i == 0)
    def _():
        tail[...] = jnp.zeros_like(tail)

    h = jnp.where(i < NFF, meta_ref[...], x_ref[0])
    y_fox = _dot(ofox_ref[0], wbf_ref[...])
    y_gdn = _dot(ogdn_ref[0], wbg_ref[...])
    gate = jax.nn.sigmoid(gates_ref[0].astype(F32) + gbias_ref[...])
    mix = gate[:, :D_MODEL] * y_fox + gate[:, D_MODEL:] * y_gdn
    h2 = h + _dot(mix.astype(BF16), wout_ref[...])
    row = lax.broadcasted_iota(jnp.int32, h2.shape, 0)
    h2 = jnp.where(i * TMF + row < PAD_ROWS, 0.0, h2)
    hn = _rms(h2, nw_ref[...]).astype(BF16)

    def conv_cols(c0, width):
        ubuf[0:8, 0:width] = tail[:, c0:c0 + width]
        ubuf[8:8 + TMF, 0:width] = _dot(hn, wup_ref[:, c0:c0 + width])
        tail[:, c0:c0 + width] = ubuf[TMF:TMF + 8, 0:width]
        w = convw_ref[:, c0:c0 + width]
        y = convb_ref[:, c0:c0 + width] + w[FFN_CONV - 1:FFN_CONV, :] * ubuf[8:8 + TMF, 0:width]
        for t in range(1, FFN_CONV):
            y = y + w[FFN_CONV - 1 - t:FFN_CONV - t, :] * ubuf[8 - t:8 - t + TMF, 0:width]
        return y

    for n in range(D_FF // FF_CHUNK):
        c0 = n * FF_CHUNK
        gate_u = conv_cols(c0, FF_CHUNK)
        up = conv_cols(D_FF + c0, FF_CHUNK)
        act[:, c0:c0 + FF_CHUNK] = (_silu(gate_u) * up).astype(BF16)

    h3 = h2 + _dot(act[...], wdown_ref[...])
    o_ref[0] = _rms(h3, fw_ref[...])


def _constants():
    qa = LANES - HEAD_DIM
    placeq = np.zeros((LANES, HEADS * qa), np.float32)
    placek = np.zeros((LANES, HEADS * LANES), np.float32)
    onesq = np.zeros((1, HEADS * qa), np.float32)
    onesk = np.zeros((1, HEADS * LANES), np.float32)
    for hd in range(HEADS):
        for part in range(3):
            placeq[part * HEADS + hd, hd * qa + A_QC - HEAD_DIM + part] = 1.0
            placek[part * HEADS + hd, hd * LANES + A_KC + part] = -1.0
        onesq[0, hd * qa + A_KC - HEAD_DIM:hd * qa + A_PAD - HEAD_DIM + 1] = 1.0
        onesk[0, hd * LANES + A_QC:hd * LANES + A_QC + 3] = 1.0
    tril_tm = np.tril(np.ones((TM, TM), np.float32))
    r = np.arange(GROWS)
    same = (r[:, None] // CHUNK) == (r[None, :] // CHUNK)
    tril_g = (same & (r[:, None] >= r[None, :])).astype(np.float32)
    last_g = (r[None, :] == (r[:, None] // CHUNK) * CHUNK + CHUNK - 1).astype(np.float32)
    eb = np.zeros((LANES, WIDTH), np.float32)
    eg = np.zeros((LANES, WIDTH), np.float32)
    for hd in range(HEADS):
        eb[L_B + hd, hd * HEAD_DIM:(hd + 1) * HEAD_DIM] = 1.0
        eg[L_A + hd, hd * HEAD_DIM:(hd + 1) * HEAD_DIM] = 1.0
    cw = np.arange(WIDTH)
    bd = ((cw[:, None] // HEAD_DIM) == (cw[None, :] // HEAD_DIM)).astype(np.float32)
    c2 = np.arange(QW)
    bdmask = ((c2[:, None] // HEAD_DIM) == (c2[None, :] // HEAD_DIM)).astype(np.float32)
    col = np.arange((GROWS // CHUNK) * WIDTH)
    tile_g = ((r[:, None] // CHUNK == col[None, :] // WIDTH)
              & (r[:, None] % CHUNK == col[None, :] % HEAD_DIM)).astype(np.float32)
    hsel = (np.arange(HEADS)[:, None] == (col[None, :] % WIDTH) // HEAD_DIM).astype(np.float32)
    return dict(placeq=placeq, placek=placek, onesq=onesq, onesk=onesk, tril_tm=tril_tm,
                tril_g=tril_g, last_g=last_g, eb=eb, eg=eg, bd=bd, bdmask=bdmask,
                tile_g=tile_g, hsel=hsel)


def _spread_heads(w):
    d = w.shape[0]
    w = w.reshape(d, HEADS, HEAD_DIM)
    return jnp.pad(w, ((0, 0), (0, 0), (0, LANES - HEAD_DIM))).reshape(d, HEADS * LANES)


def _lane_row(vals, start, copies=1):
    out = jnp.zeros((1, LANES), F32)
    for c in range(copies):
        out = lax.dynamic_update_slice(out, vals.reshape(1, -1).astype(F32), (0, start + c * vals.shape[-1]))
    return out


def kernel(x, meta_tokens, w_in, fgt_bias, gdn_conv_w, gdn_a_log, gdn_dt_bias, gdn_norm_w, gate_bias,
           w_branch_fox, w_branch_gdn, w_out, norm_mix_w, norm_ffn_w, ffn_w_up, ffn_conv_w,
           ffn_conv_b, ffn_w_down, norm_final_w):
    bn, seq, d = x.shape
    assert d == D_MODEL and seq % TQ == 0 and w_in.shape[0] == 1
    p_rows = seq + FRONT
    nt = p_rows // TM
    cst = _constants()

    w = w_in[0]
    o = 0
    wq, wk, wv = (w[:, o + j * WIDTH:o + (j + 1) * WIDTH] for j in range(3))
    o += 3 * WIDTH
    wf = w[:, o:o + HEADS]
    o += HEADS
    wg = w[:, o:o + 3 * WIDTH]
    o += 3 * WIDTH
    wz = w[:, o:o + WIDTH]
    o += WIDTH
    wb = w[:, o:o + HEADS]
    o += HEADS
    wa = w[:, o:o + HEADS]
    o += HEADS
    wgate = w[:, o:]
    ws = jnp.concatenate([wf, wf, wf, wb, wa, jnp.zeros((d, LANES - 5 * HEADS), F32)], axis=1)
    meta_tile = jnp.concatenate([jnp.zeros((PAD_ROWS, d), F32), meta_tokens.astype(F32)], axis=0)
    sbias = _lane_row(fgt_bias[0], L_F, copies=3)
    gbias = _lane_row(gdn_dt_bias[0], L_A)
    alog = _lane_row(gdn_a_log[0], L_A)
    row = lambda a: a.reshape(1, -1).astype(F32)
    b16 = lambda a: jnp.asarray(a).astype(BF16)

    cparams = lambda sem: pltpu.CompilerParams(dimension_semantics=sem, vmem_limit_bytes=VMEM_LIMIT)
    x_spec = pl.BlockSpec((1, TM, d), lambda b, i: (b, jnp.maximum(i - NF, 0), 0))
    meta_spec = pl.BlockSpec((TM, d), lambda b, i: (jnp.minimum(i, NF - 1), 0))
    tile = lambda width: pl.BlockSpec((1, TM, width), lambda b, i: (b, i, 0))

    k1_in = [x, meta_tile, row(norm_mix_w[0]), b16(wq), b16(_spread_heads(wk)), b16(wv),
             b16(ws), b16(wg), b16(wz), b16(wgate), sbias, b16(cst["placeq"]), b16(cst["placek"]),
             jnp.asarray(cst["onesq"]), jnp.asarray(cst["onesk"]), b16(cst["tril_tm"])]
    k1_specs = [x_spec, meta_spec] + [_const_spec(a.shape) for a in k1_in[2:]]
    qT, kaug, vT, gqkv, z, small, gates = pl.pallas_call(
        _inproj_kernel,
        grid=(bn, nt),
        in_specs=k1_specs,
        out_specs=[
            pl.BlockSpec((1, HEADS, 1, LANES, TM), lambda b, i: (b, 0, i // TPA, 0, i % TPA)),
            pl.BlockSpec((1, HEADS, TM, LANES), lambda b, i: (b, 0, i, 0)),
            pl.BlockSpec((1, HEADS, 1, VROWS, TM), lambda b, i: (b, 0, i // TPA, 0, i % TPA)),
            tile(3 * WIDTH), tile(WIDTH), tile(LANES), tile(2 * D_MODEL)],
        out_shape=[
            jax.ShapeDtypeStruct((bn, HEADS, p_rows // TQ, LANES, TQ), BF16),
            jax.ShapeDtypeStruct((bn, HEADS, p_rows, LANES), BF16),
            jax.ShapeDtypeStruct((bn, HEADS, p_rows // TK, VROWS, TK), BF16),
            jax.ShapeDtypeStruct((bn, p_rows, 3 * WIDTH), F32),
            jax.ShapeDtypeStruct((bn, p_rows, WIDTH), F32),
            jax.ShapeDtypeStruct((bn, p_rows, LANES), F32),
            jax.ShapeDtypeStruct((bn, p_rows, 2 * D_MODEL), BF16)],
        scratch_shapes=[pltpu.VMEM((8, LANES), F32)],
        compiler_params=cparams(("arbitrary", "arbitrary")),
        name="inproj",
    )(*k1_in)

    nq = p_rows // TQ
    o_fox = pl.pallas_call(
        _fox_kernel,
        grid=(bn, HEADS // NHA, nq),
        in_specs=[
            pl.BlockSpec((1, NHA, 1, LANES, TQ), lambda b, p, i: (b, p, i, 0, 0)),
            pl.BlockSpec((1, NHA, p_rows, LANES), lambda b, p, i: (b, p, 0, 0),
                         pipeline_mode=pl.Buffered(1)),
            pl.BlockSpec((1, NHA, p_rows // TK, VROWS, TK), lambda b, p, i: (b, p, 0, 0, 0),
                         pipeline_mode=pl.Buffered(1))],
        out_specs=pl.BlockSpec((1, TQ, NHA * HEAD_DIM), lambda b, p, i: (b, i, p)),
        out_shape=jax.ShapeDtypeStruct((bn, p_rows, WIDTH), BF16),
        scratch_shapes=[pltpu.VMEM((NHA, TK, TQ), F32), pltpu.VMEM((NHA, TK, TQ), BF16)],
        compiler_params=cparams(("arbitrary", "arbitrary", "arbitrary")),
        name="fox_attention",
    )(qT, kaug, vT)

    gtile = lambda width: pl.BlockSpec((bn, GROWS, width), lambda i: (0, i, 0))
    k3_const = [gdn_conv_w[0].astype(F32), gbias, alog, row(jnp.tile(gdn_norm_w[0], HEADS)),
                b16(cst["tril_g"]), b16(cst["last_g"]), b16(cst["eb"]), b16(cst["eg"]), b16(cst["bd"]),
                jnp.asarray(cst["bdmask"]), b16(cst["tile_g"]), jnp.asarray(cst["hsel"])]
    o_gdn = pl.pallas_call(
        _gdn_kernel,
        grid=(p_rows // GROWS,),
        in_specs=[gtile(3 * WIDTH), gtile(LANES), gtile(WIDTH)] + [_const_spec(a.shape) for a in k3_const],
        out_specs=gtile(WIDTH),
        out_shape=jax.ShapeDtypeStruct((bn, p_rows, WIDTH), BF16),
        scratch_shapes=[pltpu.VMEM((bn, GROWS + 8, 3 * WIDTH), F32),
                        pltpu.VMEM((bn, HEADS // QH, QW, QW), F32),
                        pltpu.VMEM((bn, GROWS, WIDTH), F32)],
        compiler_params=cparams(("arbitrary",)),
        name="gated_deltanet",
    )(gqkv, small, z, *k3_const)

    k4_const = [row(gate_bias[0]), b16(w_branch_fox[0]), b16(w_branch_gdn[0]), b16(w_out[0]),
                row(norm_ffn_w[0]), b16(ffn_w_up[0]), ffn_conv_w[0].astype(F32), row(ffn_conv_b[0]),
                b16(ffn_w_down[0]), row(norm_final_w)]
    ftile = lambda width: pl.BlockSpec((1, TMF, width), lambda b, i: (b, i, 0))
    out = pl.pallas_call(
        _merge_ffn_kernel,
        grid=(bn, p_rows // TMF),
        in_specs=[pl.BlockSpec((1, TMF, d), lambda b, i: (b, jnp.maximum(i - NFF, 0), 0)),
                  pl.BlockSpec((TMF, d), lambda b, i: (jnp.minimum(i, NFF - 1), 0)),
                  ftile(WIDTH), ftile(WIDTH), ftile(2 * D_MODEL)]
        + [_const_spec(a.shape) for a in k4_const],
        out_specs=pl.BlockSpec((1, TMF, d), lambda b, i: (b, jnp.maximum(i - NFF, 0), 0)),
        out_shape=jax.ShapeDtypeStruct((bn, seq, d), F32),
        scratch_shapes=[pltpu.VMEM((TMF + 8, FF_CHUNK), F32),
                        pltpu.VMEM((8, 2 * D_FF), F32),
                        pltpu.VMEM((TMF, D_FF), BF16)],
        compiler_params=cparams(("arbitrary", "arbitrary")),
        name="merge_ffn",
    )(x, meta_tile, o_fox, o_gdn, gates, *k4_const)
    return out
```

```python
import functools
import math

import jax
import jax.numpy as jnp
import numpy as np
from jax import lax
from jax.experimental import pallas as pl
from jax.experimental.pallas import tpu as pltpu

F32 = jnp.float32
BF16 = jnp.bfloat16

D_MODEL = 1024
N_META = 16
HEADS = 8
HEAD_DIM = 64
WIDTH = HEADS * HEAD_DIM
GDN_CONV = 4
D_FF = 2816
FFN_CONV = 3
CHUNK = 64
RMS_EPS = 1e-6
NEG = -1e30
LOG2E = math.log2(math.e)

LANES = 128
FRONT = 512
PAD_ROWS = FRONT - N_META
TM = 256
NF = FRONT // TM
TMF = 512
NFF = FRONT // TMF
TQ = 512
TK = 512
TPA = TQ // TM
RC = 256
NHA = 4
UNROLL = 4
VROWS = HEAD_DIM + 16
GROWS = 4 * CHUNK
QH = 4
QW = QH * HEAD_DIM
FF_CHUNK = 256
VMEM_LIMIT = 56 * 1024 * 1024

L_F = 0
L_B = 24
L_A = 32
A_QC = 64
A_KC = 67
A_PAD = 70


def _const_spec(shape):
    n = len(shape)
    return pl.BlockSpec(shape, lambda *_: (0,) * n, pipeline_mode=pl.Buffered(1))


def _split3(x):
    hi = x.astype(BF16)
    r = x - hi.astype(F32)
    mid = r.astype(BF16)
    lo = (r - mid.astype(F32)).astype(BF16)
    return hi, mid, lo


def _dot(a, b):
    return jnp.dot(a, b, preferred_element_type=F32)


def _dot_split_lhs(x, m, n):
    parts = _split3(x)[:n]
    out = _dot(parts[0], m)
    for p in parts[1:]:
        out = out + _dot(p, m)
    return out


def _dot_split_rhs(m, x, n):
    parts = _split3(x)[:n]
    out = _dot(m, parts[0])
    for p in parts[1:]:
        out = out + _dot(m, p)
    return out


def _rms(h, w):
    ms = jnp.mean(h * h, axis=-1, keepdims=True)
    return h * lax.rsqrt(ms + RMS_EPS) * w


def _log_sigmoid(x):
    return jnp.minimum(x, 0.0) - jnp.log1p(jnp.exp(-jnp.abs(x)))


def _softplus(x):
    return jnp.maximum(x, 0.0) + jnp.log1p(jnp.exp(-jnp.abs(x)))


def _silu(x):
    return x * jax.nn.sigmoid(x)


def _inproj_kernel(x_ref, meta_ref, nw_ref, wq_ref, wk_ref, wv_ref, ws_ref, wg_ref, wz_ref,
                   wgate_ref, sbias_ref, placeq_ref, placek_ref, onesq_ref, onesk_ref, tril_ref,
                   qT_ref, kaug_ref, vT_ref, gqkv_ref, z_ref, small_ref, gates_ref, carry_ref):
    i = pl.program_id(1)
    h = jnp.where(i < NF, meta_ref[...], x_ref[0])
    hn = _rms(h, nw_ref[...]).astype(BF16)

    small = _dot(hn, ws_ref[...])
    small_ref[0] = small

    @pl.when(i == 0)
    def _():
        carry_ref[...] = jnp.zeros_like(carry_ref)

    lf = _log_sigmoid(small + sbias_ref[...]) * LOG2E
    c = _dot_split_rhs(tril_ref[...], lf, 3) + carry_ref[0:1, :]
    carry_ref[...] = jnp.broadcast_to(c[TM - 1:TM, :], carry_ref.shape)
    hi = c.astype(BF16).astype(F32)
    r = c - hi
    mid = r.astype(BF16).astype(F32)
    lo = r - mid
    lane = lax.broadcasted_iota(jnp.int32, c.shape, 1)
    parts = jnp.where(lane < HEADS, hi, jnp.where(lane < 2 * HEADS, mid, lo)).astype(BF16)

    k = _dot(hn, wk_ref[...])
    aug = _dot(parts, placek_ref[...]) + onesk_ref[...]
    row = lax.broadcasted_iota(jnp.int32, aug.shape, 0)
    col = lax.broadcasted_iota(jnp.int32, aug.shape, 1)
    inert = jnp.logical_and(i * TM + row < PAD_ROWS, (col % HEAD_DIM) == A_PAD - HEAD_DIM)
    aug = jnp.where(inert, NEG, aug)
    first = lax.broadcasted_iota(jnp.int32, (TM, LANES), 1) < HEAD_DIM
    for p in range(HEADS // 2):
        kp = k[:, p * LANES:(p + 1) * LANES]
        ap = aug[:, p * LANES:(p + 1) * LANES]
        kaug_ref[0, 2 * p] = jnp.where(first, kp, ap).astype(BF16)
        kaug_ref[0, 2 * p + 1] = jnp.where(first, ap, kp).astype(BF16)

    qT = (_dot(hn, wq_ref[...]) * (HEAD_DIM ** -0.5 * LOG2E)).T
    qT = qT.reshape(HEADS, HEAD_DIM, TM).astype(BF16)
    augT = (_dot(parts, placeq_ref[...]) + onesq_ref[...]).T
    augT = augT.reshape(HEADS, LANES - HEAD_DIM, TM).astype(BF16)
    for hd in range(HEADS):
        lo_rows, hi_rows = (qT[hd], augT[hd]) if hd % 2 == 0 else (augT[hd], qT[hd])
        qT_ref[0, hd, 0, 0:HEAD_DIM, :] = lo_rows
        qT_ref[0, hd, 0, HEAD_DIM:LANES, :] = hi_rows

    vT = _dot(hn, wv_ref[...]).T
    vT_ref[0, :, 0, 0:HEAD_DIM, :] = vT.reshape(HEADS, HEAD_DIM, TM).astype(BF16)
    ones_row = lax.broadcasted_iota(jnp.int32, (HEADS, VROWS - HEAD_DIM, TM), 1) == 0
    vT_ref[0, :, 0, HEAD_DIM:VROWS, :] = ones_row.astype(F32).astype(BF16)

    gqkv_ref[0] = _dot(hn, wg_ref[...])
    z_ref[0] = _dot(hn, wz_ref[...])
    gates_ref[0] = _dot(hn, wgate_ref[...]).astype(gates_ref.dtype)


def _fox_kernel(qT_ref, k_ref, vT_ref, o_ref, s_buf, p_buf):
    i = pl.program_id(2)
    qTs = [qT_ref[0, hh, 0] for hh in range(NHA)]
    nrc = TK // RC
    chunk = lambda r: slice(r * RC, (r + 1) * RC)

    def score_chunk(j, hh, r):
        off = pl.multiple_of(j * TK + r * RC, RC)
        return _dot(k_ref[0, hh, pl.ds(off, RC), :], qTs[hh])

    cmax0 = []
    for hh in range(NHA):
        cm = jnp.full((1, TQ), NEG, F32)
        for r in range(nrc):
            sc = score_chunk(0, hh, r)
            cm = jnp.maximum(cm, jnp.max(sc, axis=0, keepdims=True))
            s_buf[hh, chunk(r), :] = sc
        cmax0.append(cm)
    p_buf[...] = jnp.zeros_like(p_buf)

    def body(j, st):
        jp = jnp.maximum(j - 1, 0)
        m_new = [jnp.maximum(st[hh][0], st[hh][3]) for hh in range(NHA)]
        alpha = [jnp.exp2(st[hh][0] - m_new[hh]) for hh in range(NHA)]
        accs = [st[hh][2] * st[hh][1] for hh in range(NHA)]
        cm_next = [jnp.full((1, TQ), NEG, F32) for _ in range(NHA)]
        for r in range(nrc):
            for hh in range(NHA):
                accs[hh] = accs[hh] + _dot(vT_ref[0, hh, jp, :, chunk(r)], p_buf[hh, chunk(r), :])
                p_buf[hh, chunk(r), :] = jnp.exp2(s_buf[hh, chunk(r), :] - m_new[hh]).astype(BF16)
                sc = score_chunk(j + 1, hh, r)
                cm_next[hh] = jnp.maximum(cm_next[hh], jnp.max(sc, axis=0, keepdims=True))
                s_buf[hh, chunk(r), :] = sc
        return tuple((m_new[hh], accs[hh], alpha[hh], cm_next[hh]) for hh in range(NHA))

    init = tuple((jnp.full((1, TQ), NEG, F32), jnp.zeros((VROWS, TQ), F32),
                  jnp.ones((1, TQ), F32), cmax0[hh]) for hh in range(NHA))
    def body_n(j0, t, n):
        for d in range(n):
            t = body(j0 + d, t)
        return t

    st = lax.fori_loop(0, i // UNROLL, lambda jj, t: body_n(UNROLL * jj, t, UNROLL), init)
    rem = i % UNROLL
    base = i - rem
    for n in range(1, UNROLL):
        st = lax.cond(rem >= n, functools.partial(body, base + n - 1), lambda t: t, st)

    causal = (lax.broadcasted_iota(jnp.int32, (TK, TQ), 0)
              <= lax.broadcasted_iota(jnp.int32, (TK, TQ), 1))
    ip = jnp.maximum(i - 1, 0)
    outs = []
    for hh in range(NHA):
        m, acc, a_prev, _ = st[hh]
        acc = a_prev * acc + _dot(vT_ref[0, hh, ip], p_buf[hh])
        sT = jnp.where(causal, s_buf[hh], NEG)
        m_new = jnp.maximum(m, jnp.max(sT, axis=0, keepdims=True))
        acc = jnp.exp2(m - m_new) * acc + _dot(vT_ref[0, hh, i], jnp.exp2(sT - m_new).astype(BF16))
        outs.append(acc[:HEAD_DIM] / acc[HEAD_DIM:HEAD_DIM + 1])
    oT = jnp.concatenate(outs, axis=0)
    o_ref[0] = oT.T.astype(o_ref.dtype)


def _nt(a, b):
    return lax.dot_general(a, b, (((1,), (1,)), ((), ())), preferred_element_type=F32)


def _tn(a, b):
    return lax.dot_general(a, b, (((0,), (0,)), ((), ())), preferred_element_type=F32)


def _gdn_kernel(gq_ref, small_ref, z_ref, convw_ref, gbias_ref, alog_ref, normw_ref,
                tril_ref, last_ref, eb_ref, eg_ref, bd_ref, bdmask_ref, tile_ref, hsel_ref,
                o_ref, convbuf, state, obuf):
    i = pl.program_id(0)
    nb = gq_ref.shape[0]

    @pl.when(i == 0)
    def _():
        convbuf[:, 0:8, :] = jnp.zeros((nb, 8, 3 * WIDTH), F32)
        state[...] = jnp.zeros_like(state)

    w = convw_ref[...]
    bd = bd_ref[...]

    def prepare(b):
        convbuf[b, 8:8 + GROWS, :] = gq_ref[b]
        xc = w[GDN_CONV - 1:GDN_CONV, :] * convbuf[b, 8:8 + GROWS, :]
        for t in range(1, GDN_CONV):
            xc = xc + w[GDN_CONV - 1 - t:GDN_CONV - t, :] * convbuf[b, 8 - t:8 - t + GROWS, :]
        convbuf[b, 0:8, :] = convbuf[b, GROWS:GROWS + 8, :]
        xs = _silu(xc)
        q = xs[:, :WIDTH]
        k = xs[:, WIDTH:2 * WIDTH]
        v = xs[:, 2 * WIDTH:]
        qn = q * lax.rsqrt(_dot_split_lhs(q * q, bd, 2) + RMS_EPS) * (HEAD_DIM ** -0.5)
        kn = k * lax.rsqrt(_dot_split_lhs(k * k, bd, 2) + RMS_EPS)
        small = small_ref[b]
        beta_e = _dot_split_lhs(jax.nn.sigmoid(small), eb_ref[...], 2)
        g = -jnp.exp(alog_ref[...]) * _softplus(small + gbias_ref[...])
        gc = _dot_split_rhs(tril_ref[...], g, 3)
        gl = _dot_split_rhs(last_ref[...], gc, 3)
        gc_e = _dot_split_lhs(gc, eg_ref[...], 3)
        gl_e = _dot_split_lhs(gl, eg_ref[...], 3)
        gcT = gc.T
        eg = jnp.exp(gc_e)
        kb = kn * beta_e
        gsel = _dot_split_lhs(gcT[L_A:L_A + HEADS, :], tile_ref[...], 3) * hsel_ref[...]
        return dict(qn=qn, kn=kn, kb=kb, kbe=kb * eg, kt=kn * jnp.exp(gl_e - gc_e), vb=v * beta_e,
                    qd=qn * eg, gc_e=gc_e, cd_e=jnp.exp(gl_e),
                    grow=jnp.sum(gsel, axis=0, keepdims=True))

    pre = [prepare(b) for b in range(nb)]

    nch = GROWS // CHUNK
    nquad = HEADS // QH
    rowi = lax.broadcasted_iota(jnp.int32, (CHUNK, QW), 0)
    posi = lax.broadcasted_iota(jnp.int32, (CHUNK, QW), 1) % HEAD_DIM
    tril = rowi >= posi
    strict = rowi > posi
    eye = (rowi == posi).astype(F32)
    bdm = bdmask_ref[...]
    bdm16 = bdm.astype(BF16)

    def bd4(x):
        return jnp.concatenate([x.astype(BF16)] * QH, axis=0) * bdm16

    def bdot(lhs, x):
        return _dot(lhs.astype(BF16), bd4(x))

    items = [(b, c, u) for b in range(nb) for c in range(nch) for u in range(nquad)]
    rows_of = lambda c: slice(c * CHUNK, (c + 1) * CHUNK)
    cols_of = lambda u: slice(u * QW, (u + 1) * QW)
    blk = lambda name, b, c, u: pre[b][name][rows_of(c), cols_of(u)]
    a_l, attn_l = {}, {}
    for (b, c, u) in items:
        grow = pre[b]["grow"][:, c * WIDTH + u * QW:c * WIDTH + (u + 1) * QW]
        decay = jnp.where(tril, jnp.exp(jnp.where(tril, blk("gc_e", b, c, u) - grow, 0.0)), 0.0)
        aa = _nt(jnp.concatenate([blk("kb", b, c, u), blk("qn", b, c, u)], axis=0).astype(BF16),
                 bd4(blk("kn", b, c, u)))
        a_l[b, c, u] = jnp.where(strict, aa[:CHUNK] * decay, 0.0)
        attn_l[b, c, u] = aa[CHUNK:] * decay

    t_l = {it: eye - a_l[it] for it in items}
    p_l = {it: bdot(a_l[it], a_l[it]) for it in items}
    steps = int(math.log2(CHUNK)) - 1
    for s in range(steps):
        for it in items:
            if s + 1 < steps:
                r = bdot(jnp.concatenate([t_l[it], p_l[it]], axis=0), p_l[it])
                t_l[it] = t_l[it] + r[:CHUNK]
                p_l[it] = r[CHUNK:]
            else:
                t_l[it] = t_l[it] + bdot(t_l[it], p_l[it])

    val_l = {it: bdot(t_l[it], blk("vb", *it)) for it in items}
    kcd_l = {it: bdot(t_l[it], blk("kbe", *it)) for it in items}

    lanes = [(b, u) for b in range(nb) for u in range(nquad)]
    s_cur = {bu: state[bu[0], bu[1]] for bu in lanes}
    for c in range(nch):
        kq = {(b, u): _dot(jnp.concatenate([kcd_l[b, c, u], blk("qd", b, c, u)], axis=0), s_cur[b, u])
              for (b, u) in lanes}
        v_new = {(b, u): val_l[b, c, u] - kq[b, u][:CHUNK] for (b, u) in lanes}
        o_l = {(b, u): kq[b, u][CHUNK:] + bdot(attn_l[b, c, u], v_new[b, u]) for (b, u) in lanes}
        upd = {(b, u): _tn(blk("kt", b, c, u), v_new[b, u]) for (b, u) in lanes}
        for (b, u) in lanes:
            cd = pre[b]["cd_e"][(c + 1) * CHUNK - 1:(c + 1) * CHUNK, cols_of(u)]
            s_cur[b, u] = (s_cur[b, u] * cd + upd[b, u]) * bdm
            obuf[b, rows_of(c), cols_of(u)] = o_l[b, u]
    for (b, u) in lanes:
        state[b, u] = s_cur[b, u]

    for b in range(nb):
        o = obuf[b]
        ms = _dot_split_lhs(o * o, bd, 2) * (1.0 / HEAD_DIM)
        o_ref[b] = (o * lax.rsqrt(ms + RMS_EPS) * normw_ref[...] * _silu(z_ref[b])).astype(o_ref.dtype)


def _merge_ffn_kernel(x_ref, meta_ref, ofox_ref, ogdn_ref, gates_ref, gbias_ref, wbf_ref, wbg_ref,
                      wout_ref, nw_ref, wup_ref, convw_ref, convb_ref, wdown_ref, fw_ref, o_ref,
                      ubuf, tail, act):
    i = pl.program_id(1)

    @pl.when(i == 0)
    def _():
        tail[...] = jnp.zeros_like(tail)

    h = jnp.where(i < NFF, meta_ref[...], x_ref[0])
    y_fox = _dot(ofox_ref[0], wbf_ref[...])
    y_gdn = _dot(ogdn_ref[0], wbg_ref[...])
    gate = jax.nn.sigmoid(gates_ref[0].astype(F32) + gbias_ref[...])
    mix = gate[:, :D_MODEL] * y_fox + gate[:, D_MODEL:] * y_gdn
    h2 = h + _dot(mix.astype(BF16), wout_ref[...])
    row = lax.broadcasted_iota(jnp.int32, h2.shape, 0)
    h2 = jnp.where(i * TMF + row < PAD_ROWS, 0.0, h2)
    hn = _rms(h2, nw_ref[...]).astype(BF16)

    def conv_cols(c0, width):
        ubuf[0:8, 0:width] = tail[:, c0:c0 + width]
        ubuf[8:8 + TMF, 0:width] = _dot(hn, wup_ref[:, c0:c0 + width])
        tail[:, c0:c0 + width] = ubuf[TMF:TMF + 8, 0:width]
        w = convw_ref[:, c0:c0 + width]
        y = convb_ref[:, c0:c0 + width] + w[FFN_CONV - 1:FFN_CONV, :] * ubuf[8:8 + TMF, 0:width]
        for t in range(1, FFN_CONV):
            y = y + w[FFN_CONV - 1 - t:FFN_CONV - t, :] * ubuf[8 - t:8 - t + TMF, 0:width]
        return y

    for n in range(D_FF // FF_CHUNK):
        c0 = n * FF_CHUNK
        gate_u = conv_cols(c0, FF_CHUNK)
        up = conv_cols(D_FF + c0, FF_CHUNK)
        act[:, c0:c0 + FF_CHUNK] = (_silu(gate_u) * up).astype(BF16)

    h3 = h2 + _dot(act[...], wdown_ref[...])
    o_ref[0] = _rms(h3, fw_ref[...])


def _constants():
    qa = LANES - HEAD_DIM
    placeq = np.zeros((LANES, HEADS * qa), np.float32)
    placek = np.zeros((LANES, HEADS * qa), np.float32)
    onesq = np.zeros((1, HEADS * qa), np.float32)
    onesk = np.zeros((1, HEADS * qa), np.float32)
    for hd in range(HEADS):
        kbase = (hd ^ 1) * qa
        for part in range(3):
            placeq[part * HEADS + hd, hd * qa + A_QC - HEAD_DIM + part] = 1.0
            placek[part * HEADS + hd, kbase + A_KC - HEAD_DIM + part] = -1.0
        onesq[0, hd * qa + A_KC - HEAD_DIM:hd * qa + A_PAD - HEAD_DIM + 1] = 1.0
        onesk[0, kbase + A_QC - HEAD_DIM:kbase + A_QC - HEAD_DIM + 3] = 1.0
    tril_tm = np.tril(np.ones((TM, TM), np.float32))
    r = np.arange(GROWS)
    same = (r[:, None] // CHUNK) == (r[None, :] // CHUNK)
    tril_g = (same & (r[:, None] >= r[None, :])).astype(np.float32)
    last_g = (r[None, :] == (r[:, None] // CHUNK) * CHUNK + CHUNK - 1).astype(np.float32)
    eb = np.zeros((LANES, WIDTH), np.float32)
    eg = np.zeros((LANES, WIDTH), np.float32)
    for hd in range(HEADS):
        eb[L_B + hd, hd * HEAD_DIM:(hd + 1) * HEAD_DIM] = 1.0
        eg[L_A + hd, hd * HEAD_DIM:(hd + 1) * HEAD_DIM] = 1.0
    cw = np.arange(WIDTH)
    bd = ((cw[:, None] // HEAD_DIM) == (cw[None, :] // HEAD_DIM)).astype(np.float32)
    c2 = np.arange(QW)
    bdmask = ((c2[:, None] // HEAD_DIM) == (c2[None, :] // HEAD_DIM)).astype(np.float32)
    col = np.arange((GROWS // CHUNK) * WIDTH)
    tile_g = ((r[:, None] // CHUNK == col[None, :] // WIDTH)
              & (r[:, None] % CHUNK == col[None, :] % HEAD_DIM)).astype(np.float32)
    hsel = (np.arange(HEADS)[:, None] == (col[None, :] % WIDTH) // HEAD_DIM).astype(np.float32)
    return dict(placeq=placeq, placek=placek, onesq=onesq, onesk=onesk, tril_tm=tril_tm,
                tril_g=tril_g, last_g=last_g, eb=eb, eg=eg, bd=bd, bdmask=bdmask,
                tile_g=tile_g, hsel=hsel)


def _lane_row(vals, start, copies=1):
    out = jnp.zeros((1, LANES), F32)
    for c in range(copies):
        out = lax.dynamic_update_slice(out, vals.reshape(1, -1).astype(F32), (0, start + c * vals.shape[-1]))
    return out


def kernel(x, meta_tokens, w_in, fgt_bias, gdn_conv_w, gdn_a_log, gdn_dt_bias, gdn_norm_w, gate_bias,
           w_branch_fox, w_branch_gdn, w_out, norm_mix_w, norm_ffn_w, ffn_w_up, ffn_conv_w,
           ffn_conv_b, ffn_w_down, norm_final_w):
    bn, seq, d = x.shape
    assert d == D_MODEL and seq % TQ == 0 and w_in.shape[0] == 1
    p_rows = seq + FRONT
    nt = p_rows // TM
    cst = _constants()

    w = w_in[0]
    o = 0
    wq, wk, wv = (w[:, o + j * WIDTH:o + (j + 1) * WIDTH] for j in range(3))
    o += 3 * WIDTH
    wf = w[:, o:o + HEADS]
    o += HEADS
    wg = w[:, o:o + 3 * WIDTH]
    o += 3 * WIDTH
    wz = w[:, o:o + WIDTH]
    o += WIDTH
    wb = w[:, o:o + HEADS]
    o += HEADS
    wa = w[:, o:o + HEADS]
    o += HEADS
    wgate = w[:, o:]
    ws = jnp.concatenate([wf, wf, wf, wb, wa, jnp.zeros((d, LANES - 5 * HEADS), F32)], axis=1)
    meta_tile = jnp.concatenate([jnp.zeros((PAD_ROWS, d), F32), meta_tokens.astype(F32)], axis=0)
    sbias = _lane_row(fgt_bias[0], L_F, copies=3)
    gbias = _lane_row(gdn_dt_bias[0], L_A)
    alog = _lane_row(gdn_a_log[0], L_A)
    row = lambda a: a.reshape(1, -1).astype(F32)
    b16 = lambda a: jnp.asarray(a).astype(BF16)

    cparams = lambda sem: pltpu.CompilerParams(dimension_semantics=sem, vmem_limit_bytes=VMEM_LIMIT)
    x_spec = pl.BlockSpec((1, TM, d), lambda b, i: (b, jnp.maximum(i - NF, 0), 0))
    meta_spec = pl.BlockSpec((TM, d), lambda b, i: (jnp.minimum(i, NF - 1), 0))
    tile = lambda width: pl.BlockSpec((1, TM, width), lambda b, i: (b, i, 0))

    k1_in = [x, meta_tile, row(norm_mix_w[0]), b16(wq), b16(wk), b16(wv),
             b16(ws), b16(wg), b16(wz), b16(wgate), sbias, b16(cst["placeq"]), b16(cst["placek"]),
             jnp.asarray(cst["onesq"]), jnp.asarray(cst["onesk"]), b16(cst["tril_tm"])]
    k1_specs = [x_spec, meta_spec] + [_const_spec(a.shape) for a in k1_in[2:]]
    qT, kaug, vT, gqkv, z, small, gates = pl.pallas_call(
        _inproj_kernel,
        grid=(bn, nt),
        in_specs=k1_specs,
        out_specs=[
            pl.BlockSpec((1, HEADS, 1, LANES, TM), lambda b, i: (b, 0, i // TPA, 0, i % TPA)),
            pl.BlockSpec((1, HEADS, TM, LANES), lambda b, i: (b, 0, i, 0)),
            pl.BlockSpec((1, HEADS, 1, VROWS, TM), lambda b, i: (b, 0, i // TPA, 0, i % TPA)),
            tile(3 * WIDTH), tile(WIDTH), tile(LANES), tile(2 * D_MODEL)],
        out_shape=[
            jax.ShapeDtypeStruct((bn, HEADS, p_rows // TQ, LANES, TQ), BF16),
            jax.ShapeDtypeStruct((bn, HEADS, p_rows, LANES), BF16),
            jax.ShapeDtypeStruct((bn, HEADS, p_rows // TK, VROWS, TK), BF16),
            jax.ShapeDtypeStruct((bn, p_rows, 3 * WIDTH), F32),
            jax.ShapeDtypeStruct((bn, p_rows, WIDTH), F32),
            jax.ShapeDtypeStruct((bn, p_rows, LANES), F32),
            jax.ShapeDtypeStruct((bn, p_rows, 2 * D_MODEL), BF16)],
        scratch_shapes=[pltpu.VMEM((8, LANES), F32)],
        compiler_params=cparams(("arbitrary", "arbitrary")),
        name="inproj",
    )(*k1_in)

    nq = p_rows // TQ
    o_fox = pl.pallas_call(
        _fox_kernel,
        grid=(bn, HEADS // NHA, nq),
        in_specs=[
            pl.BlockSpec((1, NHA, 1, LANES, TQ), lambda b, p, i: (b, p, i, 0, 0)),
            pl.BlockSpec((1, NHA, p_rows, LANES), lambda b, p, i: (b, p, 0, 0),
                         pipeline_mode=pl.Buffered(1)),
            pl.BlockSpec((1, NHA, p_rows // TK, VROWS, TK), lambda b, p, i: (b, p, 0, 0, 0),
                         pipeline_mode=pl.Buffered(1))],
        out_specs=pl.BlockSpec((1, TQ, NHA * HEAD_DIM), lambda b, p, i: (b, i, p)),
        out_shape=jax.ShapeDtypeStruct((bn, p_rows, WIDTH), BF16),
        scratch_shapes=[pltpu.VMEM((NHA, TK, TQ), F32), pltpu.VMEM((NHA, TK, TQ), BF16)],
        compiler_params=cparams(("arbitrary", "arbitrary", "arbitrary")),
        name="fox_attention",
    )(qT, kaug, vT)

    gtile = lambda width: pl.BlockSpec((bn, GROWS, width), lambda i: (0, i, 0))
    k3_const = [gdn_conv_w[0].astype(F32), gbias, alog, row(jnp.tile(gdn_norm_w[0], HEADS)),
                b16(cst["tril_g"]), b16(cst["last_g"]), b16(cst["eb"]), b16(cst["eg"]), b16(cst["bd"]),
                jnp.asarray(cst["bdmask"]), b16(cst["tile_g"]), jnp.asarray(cst["hsel"])]
    o_gdn = pl.pallas_call(
        _gdn_kernel,
        grid=(p_rows // GROWS,),
        in_specs=[gtile(3 * WIDTH), gtile(LANES), gtile(WIDTH)] + [_const_spec(a.shape) for a in k3_const],
        out_specs=gtile(WIDTH),
        out_shape=jax.ShapeDtypeStruct((bn, p_rows, WIDTH), BF16),
        scratch_shapes=[pltpu.VMEM((bn, GROWS + 8, 3 * WIDTH), F32),
                        pltpu.VMEM((bn, HEADS // QH, QW, QW), F32),
                        pltpu.VMEM((bn, GROWS, WIDTH), F32)],
        compiler_params=cparams(("arbitrary",)),
        name="gated_deltanet",
    )(gqkv, small, z, *k3_const)

    k4_const = [row(gate_bias[0]), b16(w_branch_fox[0]), b16(w_branch_gdn[0]), b16(w_out[0]),
                row(norm_ffn_w[0]), b16(ffn_w_up[0]), ffn_conv_w[0].astype(F32), row(ffn_conv_b[0]),
                b16(ffn_w_down[0]), row(norm_final_w)]
    ftile = lambda width: pl.BlockSpec((1, TMF, width), lambda b, i: (b, i, 0))
    out = pl.pallas_call(
        _merge_ffn_kernel,
        grid=(bn, p_rows // TMF),
        in_specs=[pl.BlockSpec((1, TMF, d), lambda b, i: (b, jnp.maximum(i - NFF, 0), 0)),
                  pl.BlockSpec((TMF, d), lambda b, i: (jnp.minimum(i, NFF - 1), 0)),
                  ftile(WIDTH), ftile(WIDTH), ftile(2 * D_MODEL)]
        + [_const_spec(a.shape) for a in k4_const],
        out_specs=pl.BlockSpec((1, TMF, d), lambda b, i: (b, jnp.maximum(i - NFF, 0), 0)),
        out_shape=jax.ShapeDtypeStruct((bn, seq, d), F32),
        scratch_shapes=[pltpu.VMEM((TMF + 8, FF_CHUNK), F32),
                        pltpu.VMEM((8, 2 * D_FF), F32),
                        pltpu.VMEM((TMF, D_FF), BF16)],
        compiler_params=cparams(("arbitrary", "arbitrary")),
        name="merge_ffn",
    )(x, meta_tile, o_fox, o_gdn, gates, *k4_const)
    return out
```

```python
import functools
import math

import jax
import jax.numpy as jnp
import numpy as np
from jax import lax
from jax.experimental import pallas as pl
from jax.experimental.pallas import tpu as pltpu

F32 = jnp.float32
BF16 = jnp.bfloat16

D_MODEL = 1024
N_META = 16
HEADS = 8
HEAD_DIM = 64
WIDTH = HEADS * HEAD_DIM
GDN_CONV = 4
D_FF = 2816
FFN_CONV = 3
CHUNK = 64
RMS_EPS = 1e-6
NEG = -1e30
LOG2E = math.log2(math.e)

LANES = 128
FRONT = 512
PAD_ROWS = FRONT - N_META
TM = 256
NF = FRONT // TM
TMF = 512
NFF = FRONT // TMF
TQ = 512
TK = 512
TPA = TQ // TM
RC = 256
NHA = 4
UNROLL = 4
VROWS = HEAD_DIM + 16
GROWS = 4 * CHUNK
QH = 4
QW = QH * HEAD_DIM
FF_CHUNK = 256
VMEM_LIMIT = 56 * 1024 * 1024

L_F = 0
L_B = 24
L_A = 32
A_QC = 64
A_KC = 67
A_PAD = 70


def _const_spec(shape):
    n = len(shape)
    return pl.BlockSpec(shape, lambda *_: (0,) * n, pipeline_mode=pl.Buffered(1))


def _split3(x):
    hi = x.astype(BF16)
    r = x - hi.astype(F32)
    mid = r.astype(BF16)
    lo = (r - mid.astype(F32)).astype(BF16)
    return hi, mid, lo


def _dot(a, b):
    return jnp.dot(a, b, preferred_element_type=F32)


def _dot_split_lhs(x, m, n):
    parts = _split3(x)[:n]
    out = _dot(parts[0], m)
    for p in parts[1:]:
        out = out + _dot(p, m)
    return out


def _dot_split_rhs(m, x, n):
    parts = _split3(x)[:n]
    out = _dot(m, parts[0])
    for p in parts[1:]:
        out = out + _dot(m, p)
    return out


def _rms(h, w):
    ms = jnp.mean(h * h, axis=-1, keepdims=True)
    return h * lax.rsqrt(ms + RMS_EPS) * w


def _log_sigmoid(x):
    return jnp.minimum(x, 0.0) - jnp.log1p(jnp.exp(-jnp.abs(x)))


def _softplus(x):
    return jnp.maximum(x, 0.0) + jnp.log1p(jnp.exp(-jnp.abs(x)))


def _silu(x):
    return x * jax.nn.sigmoid(x)


def _inproj_kernel(x_ref, meta_ref, nw_ref, wq_ref, wk_ref, wv_ref, ws_ref, wg_ref, wz_ref,
                   wgate_ref, sbias_ref, placeq_ref, placek_ref, onesq_ref, onesk_ref, tril_ref,
                   qT_ref, kaug_ref, vT_ref, gqkv_ref, z_ref, small_ref, gates_ref, carry_ref):
    i = pl.program_id(1)
    h = jnp.where(i < NF, meta_ref[...], x_ref[0])
    hn = _rms(h, nw_ref[...]).astype(BF16)

    small = _dot(hn, ws_ref[...])
    small_ref[0] = small

    @pl.when(i == 0)
    def _():
        carry_ref[...] = jnp.zeros_like(carry_ref)

    lf = _log_sigmoid(small + sbias_ref[...]) * LOG2E
    c = _dot_split_rhs(tril_ref[...], lf, 3) + carry_ref[0:1, :]
    carry_ref[...] = jnp.broadcast_to(c[TM - 1:TM, :], carry_ref.shape)
    hi = c.astype(BF16).astype(F32)
    r = c - hi
    mid = r.astype(BF16).astype(F32)
    lo = r - mid
    lane = lax.broadcasted_iota(jnp.int32, c.shape, 1)
    parts = jnp.where(lane < HEADS, hi, jnp.where(lane < 2 * HEADS, mid, lo)).astype(BF16)

    k = _dot(hn, wk_ref[...])
    aug = _dot(parts, placek_ref[...]) + onesk_ref[...]
    row = lax.broadcasted_iota(jnp.int32, aug.shape, 0)
    col = lax.broadcasted_iota(jnp.int32, aug.shape, 1)
    inert = jnp.logical_and(i * TM + row < PAD_ROWS, (col % HEAD_DIM) == A_PAD - HEAD_DIM)
    aug = jnp.where(inert, NEG, aug)
    first = lax.broadcasted_iota(jnp.int32, (TM, LANES), 1) < HEAD_DIM
    for p in range(HEADS // 2):
        kp = k[:, p * LANES:(p + 1) * LANES]
        ap = aug[:, p * LANES:(p + 1) * LANES]
        kaug_ref[0, 2 * p] = jnp.where(first, kp, ap).astype(BF16)
        kaug_ref[0, 2 * p + 1] = jnp.where(first, ap, kp).astype(BF16)

    qT = (_dot(hn, wq_ref[...]) * (HEAD_DIM ** -0.5 * LOG2E)).T
    qT = qT.reshape(HEADS, HEAD_DIM, TM).astype(BF16)
    augT = (_dot(parts, placeq_ref[...]) + onesq_ref[...]).T
    augT = augT.reshape(HEADS, LANES - HEAD_DIM, TM).astype(BF16)
    for hd in range(HEADS):
        lo_rows, hi_rows = (qT[hd], augT[hd]) if hd % 2 == 0 else (augT[hd], qT[hd])
        qT_ref[0, hd, 0, 0:HEAD_DIM, :] = lo_rows
        qT_ref[0, hd, 0, HEAD_DIM:LANES, :] = hi_rows

    vT = _dot(hn, wv_ref[...]).T
    vT_ref[0, :, 0, 0:HEAD_DIM, :] = vT.reshape(HEADS, HEAD_DIM, TM).astype(BF16)
    ones_row = lax.broadcasted_iota(jnp.int32, (HEADS, VROWS - HEAD_DIM, TM), 1) == 0
    vT_ref[0, :, 0, HEAD_DIM:VROWS, :] = ones_row.astype(F32).astype(BF16)

    gqkv_ref[0] = _dot(hn, wg_ref[...])
    z_ref[0] = _dot(hn, wz_ref[...])
    gates_ref[0] = _dot(hn, wgate_ref[...]).astype(gates_ref.dtype)


def _fox_kernel(qT_ref, k_ref, vT_ref, o_ref, s_buf, p_buf):
    i = pl.program_id(2)
    qTs = [qT_ref[0, hh, 0] for hh in range(NHA)]
    nrc = TK // RC
    chunk = lambda r: slice(r * RC, (r + 1) * RC)

    def score_chunk(j, hh, r):
        off = pl.multiple_of(j * TK + r * RC, RC)
        return _dot(k_ref[0, hh, pl.ds(off, RC), :], qTs[hh])

    cmax0 = []
    for hh in range(NHA):
        cm = jnp.full((1, TQ), NEG, F32)
        for r in range(nrc):
            sc = score_chunk(0, hh, r)
            cm = jnp.maximum(cm, jnp.max(sc, axis=0, keepdims=True))
            s_buf[hh, chunk(r), :] = sc
        cmax0.append(cm)
    p_buf[...] = jnp.zeros_like(p_buf)

    def body(j, st):
        jp = jnp.maximum(j - 1, 0)
        m_new = [jnp.maximum(st[hh][0], st[hh][3]) for hh in range(NHA)]
        alpha = [jnp.exp2(st[hh][0] - m_new[hh]) for hh in range(NHA)]
        accs = [st[hh][2] * st[hh][1] for hh in range(NHA)]
        cm_next = [jnp.full((1, TQ), NEG, F32) for _ in range(NHA)]
        for r in range(nrc):
            for hh in range(NHA):
                accs[hh] = accs[hh] + _dot(vT_ref[0, hh, jp, :, chunk(r)], p_buf[hh, chunk(r), :])
                p_buf[hh, chunk(r), :] = jnp.exp2(s_buf[hh, chunk(r), :] - m_new[hh]).astype(BF16)
                sc = score_chunk(j + 1, hh, r)
                cm_next[hh] = jnp.maximum(cm_next[hh], jnp.max(sc, axis=0, keepdims=True))
                s_buf[hh, chunk(r), :] = sc
        return tuple((m_new[hh], accs[hh], alpha[hh], cm_next[hh]) for hh in range(NHA))

    init = tuple((jnp.full((1, TQ), NEG, F32), jnp.zeros((VROWS, TQ), F32),
                  jnp.ones((1, TQ), F32), cmax0[hh]) for hh in range(NHA))
    def body_n(j0, t, n):
        for d in range(n):
            t = body(j0 + d, t)
        return t

    st = lax.fori_loop(0, i // UNROLL, lambda jj, t: body_n(UNROLL * jj, t, UNROLL), init)
    rem = i % UNROLL
    base = i - rem
    for n in range(1, UNROLL):
        st = lax.cond(rem >= n, functools.partial(body, base + n - 1), lambda t: t, st)

    causal = (lax.broadcasted_iota(jnp.int32, (TK, TQ), 0)
              <= lax.broadcasted_iota(jnp.int32, (TK, TQ), 1))
    ip = jnp.maximum(i - 1, 0)
    outs = []
    for hh in range(NHA):
        m, acc, a_prev, _ = st[hh]
        acc = a_prev * acc + _dot(vT_ref[0, hh, ip], p_buf[hh])
        sT = jnp.where(causal, s_buf[hh], NEG)
        m_new = jnp.maximum(m, jnp.max(sT, axis=0, keepdims=True))
        acc = jnp.exp2(m - m_new) * acc + _dot(vT_ref[0, hh, i], jnp.exp2(sT - m_new).astype(BF16))
        outs.append(acc[:HEAD_DIM] / acc[HEAD_DIM:HEAD_DIM + 1])
    oT = jnp.concatenate(outs, axis=0)
    o_ref[0] = oT.T.astype(o_ref.dtype)


def _nt(a, b):
    return lax.dot_general(a, b, (((1,), (1,)), ((), ())), preferred_element_type=F32)


def _tn(a, b):
    return lax.dot_general(a, b, (((0,), (0,)), ((), ())), preferred_element_type=F32)


def _gdn_kernel(gq_ref, small_ref, z_ref, convw_ref, gbias_ref, alog_ref, normw_ref,
                tril_ref, last_ref, eb_ref, eg_ref, bd_ref, bdmask_ref, tile_ref, hsel_ref,
                o_ref, convbuf, state, obuf):
    i = pl.program_id(0)
    nb = gq_ref.shape[0]

    @pl.when(i == 0)
    def _():
        convbuf[:, 0:8, :] = jnp.zeros((nb, 8, 3 * WIDTH), F32)
        state[...] = jnp.zeros_like(state)

    w = convw_ref[...]
    bd = bd_ref[...]

    def prepare(b):
        convbuf[b, 8:8 + GROWS, :] = gq_ref[b]
        xc = w[GDN_CONV - 1:GDN_CONV, :] * convbuf[b, 8:8 + GROWS, :]
        for t in range(1, GDN_CONV):
            xc = xc + w[GDN_CONV - 1 - t:GDN_CONV - t, :] * convbuf[b, 8 - t:8 - t + GROWS, :]
        convbuf[b, 0:8, :] = convbuf[b, GROWS:GROWS + 8, :]
        xs = _silu(xc)
        q = xs[:, :WIDTH]
        k = xs[:, WIDTH:2 * WIDTH]
        v = xs[:, 2 * WIDTH:]
        qn = q * lax.rsqrt(_dot_split_lhs(q * q, bd, 1) + RMS_EPS) * (HEAD_DIM ** -0.5)
        kn = k * lax.rsqrt(_dot_split_lhs(k * k, bd, 1) + RMS_EPS)
        small = small_ref[b]
        beta_e = _dot_split_lhs(jax.nn.sigmoid(small), eb_ref[...], 1)
        g = -jnp.exp(alog_ref[...]) * _softplus(small + gbias_ref[...])
        gc = _dot_split_rhs(tril_ref[...], g, 2)
        gl = _dot_split_rhs(last_ref[...], gc, 2)
        gc_e = _dot_split_lhs(gc, eg_ref[...], 2)
        gl_e = _dot_split_lhs(gl, eg_ref[...], 2)
        gcT = gc.T
        eg = jnp.exp(gc_e)
        kb = kn * beta_e
        gsel = _dot_split_lhs(gcT[L_A:L_A + HEADS, :], tile_ref[...], 2) * hsel_ref[...]
        return dict(qn=qn, kn=kn, kb=kb, kbe=kb * eg, kt=kn * jnp.exp(gl_e - gc_e), vb=v * beta_e,
                    qd=qn * eg, gc_e=gc_e, cd_e=jnp.exp(gl_e),
                    grow=jnp.sum(gsel, axis=0, keepdims=True))

    pre = [prepare(b) for b in range(nb)]

    nch = GROWS // CHUNK
    nquad = HEADS // QH
    rowi = lax.broadcasted_iota(jnp.int32, (CHUNK, QW), 0)
    posi = lax.broadcasted_iota(jnp.int32, (CHUNK, QW), 1) % HEAD_DIM
    tril = rowi >= posi
    strict = rowi > posi
    eye = (rowi == posi).astype(F32)
    bdm = bdmask_ref[...]
    bdm16 = bdm.astype(BF16)

    def bd4(x):
        return jnp.concatenate([x.astype(BF16)] * QH, axis=0) * bdm16

    def bdot(lhs, x):
        return _dot(lhs.astype(BF16), bd4(x))

    items = [(b, c, u) for b in range(nb) for c in range(nch) for u in range(nquad)]
    rows_of = lambda c: slice(c * CHUNK, (c + 1) * CHUNK)
    cols_of = lambda u: slice(u * QW, (u + 1) * QW)
    blk = lambda name, b, c, u: pre[b][name][rows_of(c), cols_of(u)]
    a_l, attn_l = {}, {}
    for (b, c, u) in items:
        grow = pre[b]["grow"][:, c * WIDTH + u * QW:c * WIDTH + (u + 1) * QW]
        decay = jnp.where(tril, jnp.exp(jnp.where(tril, blk("gc_e", b, c, u) - grow, 0.0)), 0.0)
        aa = _nt(jnp.concatenate([blk("kb", b, c, u), blk("qn", b, c, u)], axis=0).astype(BF16),
                 bd4(blk("kn", b, c, u)))
        a_l[b, c, u] = jnp.where(strict, aa[:CHUNK] * decay, 0.0)
        attn_l[b, c, u] = aa[CHUNK:] * decay

    t_l = {it: eye - a_l[it] for it in items}
    p_l = {it: bdot(a_l[it], a_l[it]) for it in items}
    steps = int(math.log2(CHUNK)) - 1
    for s in range(steps):
        for it in items:
            if s + 1 < steps:
                r = bdot(jnp.concatenate([t_l[it], p_l[it]], axis=0), p_l[it])
                t_l[it] = t_l[it] + r[:CHUNK]
                p_l[it] = r[CHUNK:]
            else:
                t_l[it] = t_l[it] + bdot(t_l[it], p_l[it])

    val_l = {it: bdot(t_l[it], blk("vb", *it)) for it in items}
    kcd_l = {it: bdot(t_l[it], blk("kbe", *it)) for it in items}

    lanes = [(b, u) for b in range(nb) for u in range(nquad)]
    s_cur = {bu: state[bu[0], bu[1]] for bu in lanes}
    for c in range(nch):
        kq = {(b, u): _dot(jnp.concatenate([kcd_l[b, c, u], blk("qd", b, c, u)], axis=0), s_cur[b, u])
              for (b, u) in lanes}
        v_new = {(b, u): val_l[b, c, u] - kq[b, u][:CHUNK] for (b, u) in lanes}
        o_l = {(b, u): kq[b, u][CHUNK:] + bdot(attn_l[b, c, u], v_new[b, u]) for (b, u) in lanes}
        upd = {(b, u): _tn(blk("kt", b, c, u), v_new[b, u]) for (b, u) in lanes}
        for (b, u) in lanes:
            cd = pre[b]["cd_e"][(c + 1) * CHUNK - 1:(c + 1) * CHUNK, cols_of(u)]
            s_cur[b, u] = (s_cur[b, u] * cd + upd[b, u]) * bdm
            obuf[b, rows_of(c), cols_of(u)] = o_l[b, u]
    for (b, u) in lanes:
        state[b, u] = s_cur[b, u]

    for b in range(nb):
        o = obuf[b]
        ms = _dot_split_lhs(o * o, bd, 1) * (1.0 / HEAD_DIM)
        o_ref[b] = (o * lax.rsqrt(ms + RMS_EPS) * normw_ref[...] * _silu(z_ref[b])).astype(o_ref.dtype)


def _merge_ffn_kernel(x_ref, meta_ref, ofox_ref, ogdn_ref, gates_ref, gbias_ref, wbf_ref, wbg_ref,
                      wout_ref, nw_ref, wup_ref, convw_ref, convb_ref, wdown_ref, fw_ref, o_ref,
                      ubuf, tail, act):
    i = pl.program_id(1)

    @pl.when(i == 0)
    def _():
        tail[...] = jnp.zeros_like(tail)

    h = jnp.where(i < NFF, meta_ref[...], x_ref[0])
    y_fox = _dot(ofox_ref[0], wbf_ref[...])
    y_gdn = _dot(ogdn_ref[0], wbg_ref[...])
    gate = jax.nn.sigmoid(gates_ref[0].astype(F32) + gbias_ref[...])
    mix = gate[:, :D_MODEL] * y_fox + gate[:, D_MODEL:] * y_gdn
    h2 = h + _dot(mix.astype(BF16), wout_ref[...])
    row = lax.broadcasted_iota(jnp.int32, h2.shape, 0)
    h2 = jnp.where(i * TMF + row < PAD_ROWS, 0.0, h2)
    hn = _rms(h2, nw_ref[...]).astype(BF16)

    def conv_cols(c0, width):
        ubuf[0:8, 0:width] = tail[:, c0:c0 + width]
        ubuf[8:8 + TMF, 0:width] = _dot(hn, wup_ref[:, c0:c0 + width])
        tail[:, c0:c0 + width] = ubuf[TMF:TMF + 8, 0:width]
        w = convw_ref[:, c0:c0 + width]
        y = convb_ref[:, c0:c0 + width] + w[FFN_CONV - 1:FFN_CONV, :] * ubuf[8:8 + TMF, 0:width]
        for t in range(1, FFN_CONV):
            y = y + w[FFN_CONV - 1 - t:FFN_CONV - t, :] * ubuf[8 - t:8 - t + TMF, 0:width]
        return y

    for n in range(D_FF // FF_CHUNK):
        c0 = n * FF_CHUNK
        gate_u = conv_cols(c0, FF_CHUNK)
        up = conv_cols(D_FF + c0, FF_CHUNK)
        act[:, c0:c0 + FF_CHUNK] = (_silu(gate_u) * up).astype(BF16)

    h3 = h2 + _dot(act[...], wdown_ref[...])
    o_ref[0] = _rms(h3, fw_ref[...])


def _constants():
    qa = LANES - HEAD_DIM
    placeq = np.zeros((LANES, HEADS * qa), np.float32)
    placek = np.zeros((LANES, HEADS * qa), np.float32)
    onesq = np.zeros((1, HEADS * qa), np.float32)
    onesk = np.zeros((1, HEADS * qa), np.float32)
    for hd in range(HEADS):
        kbase = (hd ^ 1) * qa
        for part in range(3):
            placeq[part * HEADS + hd, hd * qa + A_QC - HEAD_DIM + part] = 1.0
            placek[part * HEADS + hd, kbase + A_KC - HEAD_DIM + part] = -1.0
        onesq[0, hd * qa + A_KC - HEAD_DIM:hd * qa + A_PAD - HEAD_DIM + 1] = 1.0
        onesk[0, kbase + A_QC - HEAD_DIM:kbase + A_QC - HEAD_DIM + 3] = 1.0
    tril_tm = np.tril(np.ones((TM, TM), np.float32))
    r = np.arange(GROWS)
    same = (r[:, None] // CHUNK) == (r[None, :] // CHUNK)
    tril_g = (same & (r[:, None] >= r[None, :])).astype(np.float32)
    last_g = (r[None, :] == (r[:, None] // CHUNK) * CHUNK + CHUNK - 1).astype(np.float32)
    eb = np.zeros((LANES, WIDTH), np.float32)
    eg = np.zeros((LANES, WIDTH), np.float32)
    for hd in range(HEADS):
        eb[L_B + hd, hd * HEAD_DIM:(hd + 1) * HEAD_DIM] = 1.0
        eg[L_A + hd, hd * HEAD_DIM:(hd + 1) * HEAD_DIM] = 1.0
    cw = np.arange(WIDTH)
    bd = ((cw[:, None] // HEAD_DIM) == (cw[None, :] // HEAD_DIM)).astype(np.float32)
    c2 = np.arange(QW)
    bdmask = ((c2[:, None] // HEAD_DIM) == (c2[None, :] // HEAD_DIM)).astype(np.float32)
    col = np.arange((GROWS // CHUNK) * WIDTH)
    tile_g = ((r[:, None] // CHUNK == col[None, :] // WIDTH)
              & (r[:, None] % CHUNK == col[None, :] % HEAD_DIM)).astype(np.float32)
    hsel = (np.arange(HEADS)[:, None] == (col[None, :] % WIDTH) // HEAD_DIM).astype(np.float32)
    return dict(placeq=placeq, placek=placek, onesq=onesq, onesk=onesk, tril_tm=tril_tm,
                tril_g=tril_g, last_g=last_g, eb=eb, eg=eg, bd=bd, bdmask=bdmask,
                tile_g=tile_g, hsel=hsel)


def _lane_row(vals, start, copies=1):
    out = jnp.zeros((1, LANES), F32)
    for c in range(copies):
        out = lax.dynamic_update_slice(out, vals.reshape(1, -1).astype(F32), (0, start + c * vals.shape[-1]))
    return out


def kernel(x, meta_tokens, w_in, fgt_bias, gdn_conv_w, gdn_a_log, gdn_dt_bias, gdn_norm_w, gate_bias,
           w_branch_fox, w_branch_gdn, w_out, norm_mix_w, norm_ffn_w, ffn_w_up, ffn_conv_w,
           ffn_conv_b, ffn_w_down, norm_final_w):
    bn, seq, d = x.shape
    assert d == D_MODEL and seq % TQ == 0 and w_in.shape[0] == 1
    p_rows = seq + FRONT
    nt = p_rows // TM
    cst = _constants()

    w = w_in[0]
    o = 0
    wq, wk, wv = (w[:, o + j * WIDTH:o + (j + 1) * WIDTH] for j in range(3))
    o += 3 * WIDTH
    wf = w[:, o:o + HEADS]
    o += HEADS
    wg = w[:, o:o + 3 * WIDTH]
    o += 3 * WIDTH
    wz = w[:, o:o + WIDTH]
    o += WIDTH
    wb = w[:, o:o + HEADS]
    o += HEADS
    wa = w[:, o:o + HEADS]
    o += HEADS
    wgate = w[:, o:]
    ws = jnp.concatenate([wf, wf, wf, wb, wa, jnp.zeros((d, LANES - 5 * HEADS), F32)], axis=1)
    meta_tile = jnp.concatenate([jnp.zeros((PAD_ROWS, d), F32), meta_tokens.astype(F32)], axis=0)
    sbias = _lane_row(fgt_bias[0], L_F, copies=3)
    gbias = _lane_row(gdn_dt_bias[0], L_A)
    alog = _lane_row(gdn_a_log[0], L_A)
    row = lambda a: a.reshape(1, -1).astype(F32)
    b16 = lambda a: jnp.asarray(a).astype(BF16)

    cparams = lambda sem: pltpu.CompilerParams(dimension_semantics=sem, vmem_limit_bytes=VMEM_LIMIT)
    x_spec = pl.BlockSpec((1, TM, d), lambda b, i: (b, jnp.maximum(i - NF, 0), 0))
    meta_spec = pl.BlockSpec((TM, d), lambda b, i: (jnp.minimum(i, NF - 1), 0))
    tile = lambda width: pl.BlockSpec((1, TM, width), lambda b, i: (b, i, 0))

    k1_in = [x, meta_tile, row(norm_mix_w[0]), b16(wq), b16(wk), b16(wv),
             b16(ws), b16(wg), b16(wz), b16(wgate), sbias, b16(cst["placeq"]), b16(cst["placek"]),
             jnp.asarray(cst["onesq"]), jnp.asarray(cst["onesk"]), b16(cst["tril_tm"])]
    k1_specs = [x_spec, meta_spec] + [_const_spec(a.shape) for a in k1_in[2:]]
    qT, kaug, vT, gqkv, z, small, gates = pl.pallas_call(
        _inproj_kernel,
        grid=(bn, nt),
        in_specs=k1_specs,
        out_specs=[
            pl.BlockSpec((1, HEADS, 1, LANES, TM), lambda b, i: (b, 0, i // TPA, 0, i % TPA)),
            pl.BlockSpec((1, HEADS, TM, LANES), lambda b, i: (b, 0, i, 0)),
            pl.BlockSpec((1, HEADS, 1, VROWS, TM), lambda b, i: (b, 0, i // TPA, 0, i % TPA)),
            tile(3 * WIDTH), tile(WIDTH), tile(LANES), tile(2 * D_MODEL)],
        out_shape=[
            jax.ShapeDtypeStruct((bn, HEADS, p_rows // TQ, LANES, TQ), BF16),
            jax.ShapeDtypeStruct((bn, HEADS, p_rows, LANES), BF16),
            jax.ShapeDtypeStruct((bn, HEADS, p_rows // TK, VROWS, TK), BF16),
            jax.ShapeDtypeStruct((bn, p_rows, 3 * WIDTH), F32),
            jax.ShapeDtypeStruct((bn, p_rows, WIDTH), F32),
            jax.ShapeDtypeStruct((bn, p_rows, LANES), F32),
            jax.ShapeDtypeStruct((bn, p_rows, 2 * D_MODEL), BF16)],
        scratch_shapes=[pltpu.VMEM((8, LANES), F32)],
        compiler_params=cparams(("arbitrary", "arbitrary")),
        name="inproj",
    )(*k1_in)

    nq = p_rows // TQ
    o_fox = pl.pallas_call(
        _fox_kernel,
        grid=(bn, HEADS // NHA, nq),
        in_specs=[
            pl.BlockSpec((1, NHA, 1, LANES, TQ), lambda b, p, i: (b, p, i, 0, 0)),
            pl.BlockSpec((1, NHA, p_rows, LANES), lambda b, p, i: (b, p, 0, 0),
                         pipeline_mode=pl.Buffered(1)),
            pl.BlockSpec((1, NHA, p_rows // TK, VROWS, TK), lambda b, p, i: (b, p, 0, 0, 0),
                         pipeline_mode=pl.Buffered(1))],
        out_specs=pl.BlockSpec((1, TQ, NHA * HEAD_DIM), lambda b, p, i: (b, i, p)),
        out_shape=jax.ShapeDtypeStruct((bn, p_rows, WIDTH), BF16),
        scratch_shapes=[pltpu.VMEM((NHA, TK, TQ), F32), pltpu.VMEM((NHA, TK, TQ), BF16)],
        compiler_params=cparams(("arbitrary", "arbitrary", "arbitrary")),
        name="fox_attention",
    )(qT, kaug, vT)

    gtile = lambda width: pl.BlockSpec((bn, GROWS, width), lambda i: (0, i, 0))
    k3_const = [gdn_conv_w[0].astype(F32), gbias, alog, row(jnp.tile(gdn_norm_w[0], HEADS)),
                b16(cst["tril_g"]), b16(cst["last_g"]), b16(cst["eb"]), b16(cst["eg"]), b16(cst["bd"]),
                jnp.asarray(cst["bdmask"]), b16(cst["tile_g"]), jnp.asarray(cst["hsel"])]
    o_gdn = pl.pallas_call(
        _gdn_kernel,
        grid=(p_rows // GROWS,),
        in_specs=[gtile(3 * WIDTH), gtile(LANES), gtile(WIDTH)] + [_const_spec(a.shape) for a in k3_const],
        out_specs=gtile(WIDTH),
        out_shape=jax.ShapeDtypeStruct((bn, p_rows, WIDTH), BF16),
        scratch_shapes=[pltpu.VMEM((bn, GROWS + 8, 3 * WIDTH), F32),
                        pltpu.VMEM((bn, HEADS // QH, QW, QW), F32),
                        pltpu.VMEM((bn, GROWS, WIDTH), F32)],
        compiler_params=cparams(("arbitrary",)),
        name="gated_deltanet",
    )(gqkv, small, z, *k3_const)

    k4_const = [row(gate_bias[0]), b16(w_branch_fox[0]), b16(w_branch_gdn[0]), b16(w_out[0]),
                row(norm_ffn_w[0]), b16(ffn_w_up[0]), ffn_conv_w[0].astype(F32), row(ffn_conv_b[0]),
                b16(ffn_w_down[0]), row(norm_final_w)]
    ftile = lambda width: pl.BlockSpec((1, TMF, width), lambda b, i: (b, i, 0))
    out = pl.pallas_call(
        _merge_ffn_kernel,
        grid=(bn, p_rows // TMF),
        in_specs=[pl.BlockSpec((1, TMF, d), lambda b, i: (b, jnp.maximum(i - NFF, 0), 0)),
                  pl.BlockSpec((TMF, d), lambda b, i: (jnp.minimum(i, NFF - 1), 0)),
                  ftile(WIDTH), ftile(WIDTH), ftile(2 * D_MODEL)]
        + [_const_spec(a.shape) for a in k4_const],
        out_specs=pl.BlockSpec((1, TMF, d), lambda b, i: (b, jnp.maximum(i - NFF, 0), 0)),
        out_shape=jax.ShapeDtypeStruct((bn, seq, d), F32),
        scratch_shapes=[pltpu.VMEM((TMF + 8, FF_CHUNK), F32),
                        pltpu.VMEM((8, 2 * D_FF), F32),
                        pltpu.VMEM((TMF, D_FF), BF16)],
        compiler_params=cparams(("arbitrary", "arbitrary")),
        name="merge_ffn",
    )(x, meta_tile, o_fox, o_gdn, gates, *k4_const)
    return out
```
